```python
import jax, jax.numpy as jnp
from jax import lax
import numpy as np

D_MODEL = 1024
BATCH = 8
SEQ = 4096
DEPTH = 4

GRID_W = 64
Q_BLOCK = 128
ROPE_THETA = 10000.0
EPS = 1e-6

GQA_HEADS = 8
GQA_KV_HEADS = 2
GQA_GROUP = GQA_HEADS // GQA_KV_HEADS
GQA_HEAD_DIM = 64
GQA_WIDTH = GQA_HEADS * GQA_HEAD_DIM
GQA_KV_WIDTH = GQA_KV_HEADS * GQA_HEAD_DIM

MLA_HEADS = 8
MLA_Q_RANK = 256
MLA_KV_RANK = 128
MLA_NOPE_DIM = 64
MLA_ROPE_DIM = 32
MLA_V_DIM = 64
MLA_QK_DIM = MLA_NOPE_DIM + MLA_ROPE_DIM
MLA_WIDTH = MLA_HEADS * MLA_V_DIM

MIX_WIDTH = GQA_WIDTH + MLA_WIDTH
IN_SIZES = (GQA_WIDTH, GQA_KV_WIDTH, GQA_KV_WIDTH, MLA_Q_RANK, MLA_KV_RANK + MLA_ROPE_DIM)
IN_WIDTH = sum(IN_SIZES)
IN_OFFSETS = tuple(int(v) for v in np.cumsum(IN_SIZES)[:-1])

MEM_TOKENS = 256
MEM_HEADS = 4
MEM_HEAD_DIM = 128
MEM_WIDTH = MEM_HEADS * MEM_HEAD_DIM

N_EXPERTS = 16
EXPERT_FF = 512
EC_CAPACITY_FACTOR = 2

kernel_name = "hybrid_gqa_mla_memxattn_ecmoe_encoder"


def rms_norm(x, g):
    xf = x.astype(jnp.float32)
    y = xf * lax.rsqrt(jnp.mean(xf * xf, axis=-1, keepdims=True) + EPS)
    return (y * g.astype(jnp.float32)).astype(x.dtype)


def axial_rope_angles(seq_len, rot_dim):
    rows = seq_len // GRID_W
    row = jnp.repeat(jnp.arange(rows, dtype=jnp.float32), GRID_W)
    col = jnp.tile(jnp.arange(GRID_W, dtype=jnp.float32), rows)
    axis_dim = rot_dim // 2
    inv_freq = ROPE_THETA ** (-jnp.arange(0, axis_dim, 2, dtype=jnp.float32) / axis_dim)
    ang = jnp.concatenate([row[:, None] * inv_freq[None, :], col[:, None] * inv_freq[None, :]], axis=-1)
    return jnp.cos(ang), jnp.sin(ang)


def apply_rope(x, cos, sin):
    c = cos[None, :, None, :].astype(x.dtype)
    s = sin[None, :, None, :].astype(x.dtype)
    half = x.shape[-1] // 2
    x1, x2 = x[..., :half], x[..., half:]
    return jnp.concatenate([x1 * c - x2 * s, x2 * c + x1 * s], axis=-1)


def blocked_attention(q, k, v, scale):
    B, S, K, G, Dk = q.shape
    Dv = v.shape[-1]
    nblk = S // Q_BLOCK
    qb = q.reshape(B, nblk, Q_BLOCK, K, G, Dk).transpose(1, 0, 2, 3, 4, 5)

    def one_block(qblk):
        s = jnp.einsum('bqkgd,btkd->bkgqt', qblk, k).astype(jnp.float32) * scale
        p = jax.nn.softmax(s, axis=-1).astype(v.dtype)
        return jnp.einsum('bkgqt,btkd->bqkgd', p, v)

    o = lax.map(one_block, qb)
    return o.transpose(1, 0, 2, 3, 4, 5).reshape(B, S, K * G * Dv)


def parallel_mixer(h, w_in, gqa_q_norm, gqa_k_norm, mla_q_norm, mla_kv_norm, w_q_b, w_kv_b,
                   out_norm_gqa, out_norm_mla, w_o, cos_g, sin_g, cos_m, sin_m):
    B, S, _ = h.shape
    proj = h @ w_in
    q_g, k_g, v_g, q_lat, kv_lat = jnp.split(proj, IN_OFFSETS, axis=-1)

    q_g = apply_rope(rms_norm(q_g.reshape(B, S, GQA_HEADS, GQA_HEAD_DIM), gqa_q_norm), cos_g, sin_g)
    k_g = apply_rope(rms_norm(k_g.reshape(B, S, GQA_KV_HEADS, GQA_HEAD_DIM), gqa_k_norm), cos_g, sin_g)
    v_g = v_g.reshape(B, S, GQA_KV_HEADS, GQA_HEAD_DIM)
    o_g = blocked_attention(q_g.reshape(B, S, GQA_KV_HEADS, GQA_GROUP, GQA_HEAD_DIM), k_g, v_g,
                            GQA_HEAD_DIM ** -0.5)

    c_q = rms_norm(q_lat, mla_q_norm)
    q_m = (c_q @ w_q_b).reshape(B, S, MLA_HEADS, MLA_QK_DIM)
    q_m = jnp.concatenate([q_m[..., :MLA_NOPE_DIM], apply_rope(q_m[..., MLA_NOPE_DIM:], cos_m, sin_m)], axis=-1)
    c_kv = rms_norm(kv_lat[..., :MLA_KV_RANK], mla_kv_norm)
    k_rope = apply_rope(kv_lat[..., MLA_KV_RANK:].reshape(B, S, 1, MLA_ROPE_DIM), cos_m, sin_m)
    kv = (c_kv @ w_kv_b).reshape(B, S, MLA_HEADS, MLA_NOPE_DIM + MLA_V_DIM)
    k_m = jnp.concatenate([kv[..., :MLA_NOPE_DIM],
                           jnp.broadcast_to(k_rope, (B, S, MLA_HEADS, MLA_ROPE_DIM))], axis=-1)
    v_m = kv[..., MLA_NOPE_DIM:]
    o_m = blocked_attention(q_m[:, :, :, None, :], k_m, v_m, MLA_QK_DIM ** -0.5)

    merged = jnp.concatenate([rms_norm(o_g, out_norm_gqa), rms_norm(o_m, out_norm_mla)], axis=-1)
    return merged @ w_o


def memory_cross_attention(h, m, w_mem_q, w_mem_kv, w_mem_o):
    B, S, _ = h.shape
    M = m.shape[1]
    q = (h @ w_mem_q).reshape(B, S, MEM_HEADS, MEM_HEAD_DIM)
    kv = (m @ w_mem_kv).reshape(B, M, 2, MEM_HEADS, MEM_HEAD_DIM)
    k, v = kv[:, :, 0], kv[:, :, 1]
    s = jnp.einsum('bshd,bmhd->bhsm', q, k).astype(jnp.float32) * (MEM_HEAD_DIM ** -0.5)
    p = jax.nn.softmax(s, axis=-1).astype(v.dtype)
    o = jnp.einsum('bhsm,bmhd->bshd', p, v).reshape(B, S, MEM_WIDTH)
    return o @ w_mem_o


def expert_choice_moe(h, w_router, w_gate, w_up, w_down):
    B, S, D = h.shape
    cap = EC_CAPACITY_FACTOR * S // N_EXPERTS
    affinity = jax.nn.softmax((h @ w_router).astype(jnp.float32), axis=-1)
    gate, idx = lax.top_k(jnp.swapaxes(affinity, 1, 2), cap)
    x_in = jax.vmap(lambda hb, ib: hb[ib])(h, idx)
    a = jnp.einsum('becd,edf->becf', x_in, w_gate)
    u = jnp.einsum('becd,edf->becf', x_in, w_up)
    y = jnp.einsum('becf,efd->becd', jax.nn.silu(a) * u, w_down)
    y = y * gate[..., None].astype(y.dtype)
    return jax.vmap(lambda ib, yb: jnp.zeros((S, D), yb.dtype).at[ib.reshape(-1)].add(yb.reshape(-1, D)))(idx, y)


def setup_inputs(seed: int = 0) -> dict:
    key = jax.random.key(seed)
    ks = iter(jax.random.split(key, 32))
    L = DEPTH

    def w(shape, fan_in):
        return jax.random.normal(next(ks), shape, jnp.float32) * fan_in ** -0.5

    def gain(shape):
        return 1.0 + 0.02 * jax.random.normal(next(ks), shape, jnp.float32)

    return {
        "x": jax.random.normal(next(ks), (BATCH, SEQ, D_MODEL), jnp.float32),
        "mem": jax.random.normal(next(ks), (BATCH, MEM_TOKENS, D_MODEL), jnp.float32),
        "ln_mix": gain((L, D_MODEL)),
        "w_in": w((L, D_MODEL, IN_WIDTH), D_MODEL),
        "gqa_q_norm": gain((L, GQA_HEAD_DIM)),
        "gqa_k_norm": gain((L, GQA_HEAD_DIM)),
        "mla_q_norm": gain((L, MLA_Q_RANK)),
        "mla_kv_norm": gain((L, MLA_KV_RANK)),
        "w_q_b": w((L, MLA_Q_RANK, MLA_HEADS * MLA_QK_DIM), MLA_Q_RANK),
        "w_kv_b": w((L, MLA_KV_RANK, MLA_HEADS * (MLA_NOPE_DIM + MLA_V_DIM)), MLA_KV_RANK),
        "out_norm_gqa": gain((L, GQA_WIDTH)),
        "out_norm_mla": gain((L, MLA_WIDTH)),
        "w_o": w((L, MIX_WIDTH, D_MODEL), MIX_WIDTH),
        "ln_mem": gain((L, D_MODEL)),
        "ln_mem_kv": gain((L, D_MODEL)),
        "w_mem_q": w((L, D_MODEL, MEM_WIDTH), D_MODEL),
        "w_mem_kv": w((L, D_MODEL, 2 * MEM_WIDTH), D_MODEL),
        "w_mem_o": w((L, MEM_WIDTH, D_MODEL), MEM_WIDTH),
        "ln_ffn": gain((L, D_MODEL)),
        "w_router": w((L, D_MODEL, N_EXPERTS), D_MODEL),
        "w_gate": w((L, N_EXPERTS, D_MODEL, EXPERT_FF), D_MODEL),
        "w_up": w((L, N_EXPERTS, D_MODEL, EXPERT_FF), D_MODEL),
        "w_down": w((L, N_EXPERTS, EXPERT_FF, D_MODEL), EXPERT_FF),
        "ln_final": gain((D_MODEL,)),
    }


def reference(x, mem, ln_mix, w_in, gqa_q_norm, gqa_k_norm, mla_q_norm, mla_kv_norm, w_q_b, w_kv_b,
              out_norm_gqa, out_norm_mla, w_o, ln_mem, ln_mem_kv, w_mem_q, w_mem_kv, w_mem_o,
              ln_ffn, w_router, w_gate, w_up, w_down, ln_final):
    S = x.shape[1]
    cos_g, sin_g = axial_rope_angles(S, GQA_HEAD_DIM)
    cos_m, sin_m = axial_rope_angles(S, MLA_ROPE_DIM)
    for l in range(DEPTH):
        h = rms_norm(x, ln_mix[l])
        x = x + parallel_mixer(h, w_in[l], gqa_q_norm[l], gqa_k_norm[l], mla_q_norm[l], mla_kv_norm[l],
                               w_q_b[l], w_kv_b[l], out_norm_gqa[l], out_norm_mla[l], w_o[l],
                               cos_g, sin_g, cos_m, sin_m)
        h = rms_norm(x, ln_mem[l])
        x = x + memory_cross_attention(h, rms_norm(mem, ln_mem_kv[l]), w_mem_q[l], w_mem_kv[l], w_mem_o[l])
        h = rms_norm(x, ln_ffn[l])
        x = x + expert_choice_moe(h, w_router[l], w_gate[l], w_up[l], w_down[l])
    return rms_norm(x, ln_final)
```

```python
import functools
import math

import numpy as np
import jax
import jax.numpy as jnp
from jax import lax
from jax.experimental import pallas as pl
from jax.experimental.pallas import tpu as pltpu

F32 = jnp.float32
BF16 = jnp.bfloat16

GRID_W = 64
ROPE_THETA = 10000.0
EPS = 1e-6
GQA_HEADS = 8
GQA_KV_HEADS = 2
GQA_GROUP = GQA_HEADS // GQA_KV_HEADS
GQA_HEAD_DIM = 64
MLA_HEADS = 8
MLA_Q_RANK = 256
MLA_KV_RANK = 128
MLA_NOPE_DIM = 64
MLA_ROPE_DIM = 32
MLA_V_DIM = 64
MLA_QK_DIM = MLA_NOPE_DIM + MLA_ROPE_DIM
MEM_HEADS = 4
MEM_HEAD_DIM = 128
N_EXPERTS = 16
EC_CAPACITY_FACTOR = 2

LANES = 128
LOG2E = math.log2(math.e)
VMEM_LIMIT = 56 * 1024 * 1024

N_HEADS = GQA_HEADS + MLA_HEADS
N_PAIRS = N_HEADS // 2
GQA_PAIRS = GQA_HEADS // 2


def _cparams(n_axes):
    return pltpu.CompilerParams(dimension_semantics=("arbitrary",) * n_axes, vmem_limit_bytes=VMEM_LIMIT)


def _rms(x, eps=EPS):
    return x * lax.rsqrt(jnp.mean(x * x, axis=-1, keepdims=True) + eps)


def _dot(a, b):
    return jnp.dot(a, b, preferred_element_type=F32)


def _dot_nt(a, b):
    return lax.dot_general(a, b, (((1,), (1,)), ((), ())), preferred_element_type=F32)


_ONE_LANE_A = 64
_ONE_LANE_B = 0


def _zeros_like_cols(w, n):
    return jnp.zeros(w.shape[:-1] + (n,), w.dtype)


def _lay_gqa(w):
    z = _zeros_like_cols(w, 32)
    return jnp.concatenate([w[..., :32], z, w[..., 32:], z], axis=-1)


def _lay_mla(nope, rope):
    z = _zeros_like_cols(nope, 16)
    return jnp.concatenate([nope[..., :32], rope[..., :16], z, nope[..., 32:], rope[..., 16:], z], axis=-1)


def _lay_v(v, second):
    z = _zeros_like_cols(v, 64)
    return jnp.concatenate([z, v] if second else [v, z], axis=-1)


def _rope_tables(seq_len):
    rows = seq_len // GRID_W
    row = jnp.repeat(jnp.arange(rows, dtype=F32), GRID_W)
    col = jnp.tile(jnp.arange(GRID_W, dtype=F32), rows)

    def angles(rot_dim):
        axis_dim = rot_dim // 2
        inv_freq = ROPE_THETA ** (-jnp.arange(0, axis_dim, 2, dtype=F32) / axis_dim)
        ang = jnp.concatenate([row[:, None] * inv_freq[None, :], col[:, None] * inv_freq[None, :]], axis=-1)
        return jnp.cos(ang), jnp.sin(ang)

    cg, sg = angles(GQA_HEAD_DIM)
    cm, sm = angles(MLA_ROPE_DIM)
    cos_g = _lay_gqa(jnp.concatenate([cg, cg], axis=-1))
    sin_g = _lay_gqa(jnp.concatenate([-sg, sg], axis=-1))
    one = jnp.ones((seq_len, MLA_NOPE_DIM), F32)
    cos_m = _lay_mla(one, jnp.concatenate([cm, cm], axis=-1))
    sin_m = _lay_mla(0.0 * one, jnp.concatenate([-sm, sm], axis=-1))
    return cos_g, sin_g, cos_m, sin_m


_W_SPLITS = (GQA_HEADS * LANES, GQA_KV_HEADS * LANES, 2 * GQA_KV_HEADS * LANES, MLA_Q_RANK, MLA_KV_RANK, LANES)
_W_OFFS = tuple(int(v) for v in np.cumsum((0,) + _W_SPLITS))


def _mixer_in_kernel(x_ref, ln_ref, wcat_ref, wqb_ref, wkb_ref, wvb_ref, gq_ref, gk_ref, gql_ref, gkv_ref,
                     cg_ref, sg_ref, cmq_ref, smq_ref, cmk_ref, smk_ref, oneg_ref, onem_ref,
                     q_ref, k_ref, v_ref):
    x = x_ref[0]
    h = (_rms(x) * ln_ref[...]).astype(BF16)
    proj = _dot(h, wcat_ref[...])
    o = _W_OFFS
    cg, sg = cg_ref[...], sg_ref[...]

    def head_norm_rope(blk, gain):
        ss = jnp.sum(blk * blk, axis=-1, keepdims=True) * (1.0 / GQA_HEAD_DIM)
        y = blk * lax.rsqrt(ss + EPS) * gain
        return y * cg + pltpu.roll(y, 64, 1) * sg

    for j in range(GQA_HEADS):
        blk = proj[:, o[0] + j * LANES:o[0] + (j + 1) * LANES]
        q_ref[0, :, j * LANES:(j + 1) * LANES] = head_norm_rope(blk, gq_ref[...]).astype(BF16)
    for j in range(GQA_KV_HEADS):
        blk = proj[:, o[1] + j * LANES:o[1] + (j + 1) * LANES]
        k_ref[0, :, j * LANES:(j + 1) * LANES] = head_norm_rope(blk, gk_ref[...]).astype(BF16)
    n_vg = 2 * GQA_KV_HEADS * LANES
    v_ref[0, :, 0:n_vg] = (proj[:, o[2]:o[3]] + oneg_ref[...]).astype(BF16)

    c_q = (_rms(proj[:, o[3]:o[4]]) * gql_ref[...]).astype(BF16)
    qm = _dot(c_q, wqb_ref[...])
    cmq, smq = cmq_ref[...], smq_ref[...]
    for j in range(MLA_HEADS):
        blk = qm[:, j * LANES:(j + 1) * LANES]
        q_ref[0, :, (GQA_HEADS + j) * LANES:(GQA_HEADS + j + 1) * LANES] = (
            blk * cmq + pltpu.roll(blk, 64, 1) * smq).astype(BF16)

    c_kv = (_rms(proj[:, o[4]:o[5]]) * gkv_ref[...]).astype(BF16)
    kn = _dot(c_kv, wkb_ref[...])
    vm = _dot(c_kv, wvb_ref[...]) + onem_ref[...]
    kr = proj[:, o[5]:o[6]]
    kr = kr * cmk_ref[...] + pltpu.roll(kr, 64, 1) * smk_ref[...]
    for j in range(MLA_HEADS):
        k_ref[0, :, (GQA_KV_HEADS + j) * LANES:(GQA_KV_HEADS + j + 1) * LANES] = (
            kn[:, j * LANES:(j + 1) * LANES] + kr).astype(BF16)
    v_ref[0, :, n_vg:] = vm.astype(BF16)


def _mixer_in(x, lw, tabs, ts):
    B, S, D = x.shape
    nq, nk, nv = N_HEADS * LANES, (GQA_KV_HEADS + MLA_HEADS) * LANES, (2 * GQA_KV_HEADS + MLA_HEADS) * LANES
    full = lambda a: pl.BlockSpec(a.shape, lambda b, i: (0,) * a.ndim)
    tab = pl.BlockSpec((ts, LANES), lambda b, i: (i, 0))
    consts = (lw["ln_mix"], lw["wcat"], lw["wqb"], lw["wkb"], lw["wvb"], lw["gq"], lw["gk"], lw["gql"], lw["gkv"])
    return pl.pallas_call(
        _mixer_in_kernel,
        grid=(B, S // ts),
        in_specs=[pl.BlockSpec((1, ts, D), lambda b, i: (b, i, 0))] + [full(a) for a in consts]
        + [tab] * 6 + [full(tabs["one_g"]), full(tabs["one_m"])],
        out_specs=[pl.BlockSpec((1, ts, nq), lambda b, i: (b, i, 0)),
                   pl.BlockSpec((1, ts, nk), lambda b, i: (b, i, 0)),
                   pl.BlockSpec((1, ts, nv), lambda b, i: (b, i, 0))],
        out_shape=[jax.ShapeDtypeStruct((B, S, nq), BF16), jax.ShapeDtypeStruct((B, S, nk), BF16),
                   jax.ShapeDtypeStruct((B, S, nv), BF16)],
        compiler_params=_cparams(2),
        name="mixer_in",
    )(x, *consts, tabs["cos_g"], tabs["sin_g"], tabs["cos_mq"], tabs["sin_mq"], tabs["cos_m"], tabs["sin_m"],
      tabs["one_g"], tabs["one_m"])


def _attn_kernel(q_ref, ka_ref, kb_ref, va_ref, vb_ref, o_ref, *, tk):
    tq = q_ref.shape[1]
    n_chunks = ka_ref.shape[1] // tk

    def one_head(q, k_ref, v_ref):
        def body(c, carry):
            m, acc = carry
            start = pl.multiple_of(c * tk, tk)
            ks = k_ref[0, pl.ds(start, tk), :]
            vs = v_ref[0, pl.ds(start, tk), :]
            s = _dot_nt(q, ks)
            m_new = jnp.maximum(m, jnp.max(s, axis=-1, keepdims=True))
            alpha = jnp.exp2(m - m_new)
            p = jnp.exp2(s - m_new).astype(BF16)
            return m_new, alpha * acc + _dot(p, vs)

        m0 = jnp.full((tq, 1), -jnp.inf, F32)
        acc0 = jnp.zeros((tq, LANES), F32)
        return lax.fori_loop(0, n_chunks, body, (m0, acc0))[1]

    acc_a = one_head(q_ref[0, :, 0:LANES], ka_ref, va_ref)
    acc_b = one_head(q_ref[0, :, LANES:2 * LANES], kb_ref, vb_ref)
    out_a = acc_a * (1.0 / acc_a[:, _ONE_LANE_A:_ONE_LANE_A + 1])
    out_b = acc_b * (1.0 / acc_b[:, _ONE_LANE_B:_ONE_LANE_B + 1])
    lane = lax.broadcasted_iota(jnp.int32, (tq, LANES), 1)
    o_ref[0] = jnp.where(lane < 64, out_a, out_b)


def _attention(q_all, k_all, v_all, tq, tk):
    B, S, _ = q_all.shape
    gp = GQA_PAIRS
    pairs_per_kv = GQA_GROUP // 2

    def k_col(p, second):
        return jnp.where(p < gp, p // pairs_per_kv, GQA_KV_HEADS + 2 * (p - gp) + second)

    def v_col(p, second):
        return jnp.where(p < gp, 2 * (p // pairs_per_kv) + second, 2 * GQA_KV_HEADS + 2 * (p - gp) + second)

    kv_spec = lambda col, second: pl.BlockSpec((1, S, LANES), lambda b, p, i: (b, 0, col(p, second)))
    return pl.pallas_call(
        functools.partial(_attn_kernel, tk=tk),
        grid=(B, N_PAIRS, S // tq),
        in_specs=[pl.BlockSpec((1, tq, 2 * LANES), lambda b, p, i: (b, i, p)),
                  kv_spec(k_col, 0), kv_spec(k_col, 1), kv_spec(v_col, 0), kv_spec(v_col, 1)],
        out_specs=pl.BlockSpec((1, tq, LANES), lambda b, p, i: (b, i, p)),
        out_shape=jax.ShapeDtypeStruct((B, S, N_PAIRS * LANES), F32),
        compiler_params=_cparams(3),
        name="attention",
    )(q_all, k_all, k_all, v_all, v_all)


def _mem_kv_kernel(m_ref, ln_ref, w_ref, o_ref):
    h = (_rms(m_ref[0]) * ln_ref[...]).astype(BF16)
    o_ref[0] = _dot(h, w_ref[...]).astype(BF16)


def _mem_kv(mem, ln, w):
    B, M, D = mem.shape
    n = w.shape[1]
    return pl.pallas_call(
        _mem_kv_kernel,
        grid=(B,),
        in_specs=[pl.BlockSpec((1, M, D), lambda b: (b, 0, 0)), pl.BlockSpec(ln.shape, lambda b: (0, 0)),
                  pl.BlockSpec(w.shape, lambda b: (0, 0))],
        out_specs=pl.BlockSpec((1, M, n), lambda b: (b, 0, 0)),
        out_shape=jax.ShapeDtypeStruct((B, M, n), BF16),
        compiler_params=_cparams(1),
        name="mem_kv",
    )(mem, ln, w)


def _post_attn_kernel(o_ref, x_ref, on_ref, wo_ref, lnm_ref, wmq_ref, kv_ref, wmo_ref, lnf_ref, wr_ref,
                      x2_ref, h3_ref, aff_ref):
    o = o_ref[0]
    half = o.shape[1] // 2
    merged = (jnp.concatenate([_rms(o[:, :half]), _rms(o[:, half:])], axis=-1) * on_ref[...]).astype(BF16)
    x1 = x_ref[0] + _dot(merged, wo_ref[...])

    h2 = (_rms(x1) * lnm_ref[...]).astype(BF16)
    q = (_dot(h2, wmq_ref[...]) * (MEM_HEAD_DIM ** -0.5 * LOG2E)).astype(BF16)
    kv = kv_ref[0]
    n_mem = MEM_HEADS * MEM_HEAD_DIM
    outs = []
    for hh in range(MEM_HEADS):
        lo, hi = hh * MEM_HEAD_DIM, (hh + 1) * MEM_HEAD_DIM
        s = _dot_nt(q[:, lo:hi], kv[:, lo:hi])
        p = jnp.exp2(s - jnp.max(s, axis=-1, keepdims=True))
        l = jnp.sum(p, axis=-1, keepdims=True)
        outs.append(_dot(p.astype(BF16), kv[:, n_mem + lo:n_mem + hi]) * (1.0 / l))
    oc = jnp.concatenate(outs, axis=-1).astype(BF16)
    x2 = x1 + _dot(oc, wmo_ref[...])
    x2_ref[0] = x2

    h3 = _rms(x2) * lnf_ref[...]
    h3_ref[0] = h3.astype(BF16)
    logits = lax.dot_general(wr_ref[...], h3, (((1,), (1,)), ((), ())), preferred_element_type=F32,
                             precision=lax.Precision.HIGHEST)
    e = jnp.exp(logits - jnp.max(logits, axis=0, keepdims=True))
    aff_ref[0] = e * (1.0 / jnp.sum(e, axis=0, keepdims=True))


def _post_attn(o, x, kv_mem, lw, ts):
    B, S, D = x.shape
    E = N_EXPERTS
    consts_a = (lw["on"], lw["w_o"], lw["ln_mem"], lw["w_mem_q"])
    consts_b = (lw["w_mem_o"], lw["ln_ffn"], lw["w_router_t"])
    full = lambda a: pl.BlockSpec(a.shape, lambda b, i: (0,) * a.ndim)
    tile = lambda n: pl.BlockSpec((1, ts, n), lambda b, i: (b, i, 0))
    return pl.pallas_call(
        _post_attn_kernel,
        grid=(B, S // ts),
        in_specs=[tile(o.shape[2]), tile(D)] + [full(a) for a in consts_a]
        + [pl.BlockSpec((1,) + kv_mem.shape[1:], lambda b, i: (b, 0, 0))] + [full(a) for a in consts_b],
        out_specs=[tile(D), tile(D), pl.BlockSpec((1, E, ts), lambda b, i: (b, 0, i))],
        out_shape=[jax.ShapeDtypeStruct((B, S, D), F32), jax.ShapeDtypeStruct((B, S, D), BF16),
                   jax.ShapeDtypeStruct((B, E, S), F32)],
        compiler_params=_cparams(2),
        name="post_attn",
    )(o, x, *consts_a, kv_mem, *consts_b)


def _select_kernel(aff_ref, slot_ref, *, cap, chunks):
    a = aff_ref[0]
    rows = a.shape[0]
    bits = pltpu.bitcast(a, jnp.int32)

    r_i = lax.broadcasted_iota(jnp.int32, (rows, rows), 0)
    c_i = lax.broadcasted_iota(jnp.int32, (rows, rows), 1)
    same = (r_i // chunks) == (c_i // chunks)
    bd_all = jnp.where(same, 1.0, 0.0).astype(BF16)
    bd_before = jnp.where(same & (c_i < r_i), 1.0, 0.0).astype(BF16)
    l_r = lax.broadcasted_iota(jnp.int32, (LANES, LANES), 0)
    l_c = lax.broadcasted_iota(jnp.int32, (LANES, LANES), 1)
    ones = jnp.ones((LANES, LANES), BF16)
    before = jnp.where(l_r < l_c, 1.0, 0.0).astype(BF16)

    def as01(mask):
        return jnp.where(mask, 1.0, 0.0).astype(BF16)

    def expert_count(x01):
        return _dot(bd_all, _dot(x01, ones).astype(BF16))

    def prefix_excl(x01):
        return _dot(x01, before) + _dot(bd_before, _dot(x01, ones).astype(BF16))

    def step(i, theta):
        cand = theta | (jnp.int32(1) << (30 - i))
        cnt = expert_count(as01(bits >= cand))
        return jnp.where(cnt >= cap, cand, theta)

    theta = lax.fori_loop(0, 31, step, jnp.zeros(bits.shape, jnp.int32))
    gt = bits > theta
    eq = bits == theta
    need = cap - expert_count(as01(gt))
    sel = gt | (eq & (prefix_excl(as01(eq)) < need))
    pos = prefix_excl(as01(sel))
    slot_ref[0] = jnp.where(sel, pos, -1.0).astype(jnp.int32)


def _select(aff2, cap, chunks):
    B, rows, _ = aff2.shape
    return pl.pallas_call(
        functools.partial(_select_kernel, cap=cap, chunks=chunks),
        grid=(B,),
        in_specs=[pl.BlockSpec((1, rows, LANES), lambda b: (b, 0, 0))],
        out_specs=pl.BlockSpec((1, rows, LANES), lambda b: (b, 0, 0)),
        out_shape=jax.ShapeDtypeStruct((B, rows, LANES), jnp.int32),
        compiler_params=_cparams(1),
        name="select",
    )(aff2)


def _expert_kernel(slot_ref, aff_ref, h_ref, wg_ref, wu_ref, wd_ref, y_ref, *, cap):
    slot = slot_ref[0, 0]
    S = slot.shape[1]
    hit = lax.broadcasted_iota(jnp.int32, (cap, S), 0) == slot
    gate = jnp.sum(jnp.where(hit, aff_ref[0, 0], 0.0), axis=-1, keepdims=True)
    onehot = jnp.where(hit, 1.0, 0.0).astype(BF16)
    x_in = _dot(onehot, h_ref[0]).astype(BF16)
    a = _dot(x_in, wg_ref[0])
    u = _dot(x_in, wu_ref[0])
    hm = (a * (1.0 / (1.0 + jnp.exp(-a))) * u).astype(BF16)
    y_ref[0, 0] = (_dot(hm, wd_ref[0]) * gate).astype(BF16)


def _experts(slot_row, aff_row, h3, wg, wu, wd, cap):
    B, S, D = h3.shape
    E = N_EXPERTS
    F = wg.shape[2]
    row = pl.BlockSpec((1, 1, 1, S), lambda b, e: (b, e, 0, 0))
    return pl.pallas_call(
        functools.partial(_expert_kernel, cap=cap),
        grid=(B, E),
        in_specs=[row, row, pl.BlockSpec((1, S, D), lambda b, e: (b, 0, 0)),
                  pl.BlockSpec((1, D, F), lambda b, e: (e, 0, 0)), pl.BlockSpec((1, D, F), lambda b, e: (e, 0, 0)),
                  pl.BlockSpec((1, F, D), lambda b, e: (e, 0, 0))],
        out_specs=pl.BlockSpec((1, 1, cap, D), lambda b, e: (b, e, 0, 0)),
        out_shape=jax.ShapeDtypeStruct((B, E, cap, D), BF16),
        compiler_params=_cparams(2),
        name="experts",
    )(slot_row, aff_row, h3, wg, wu, wd)


def _combine_kernel(x_ref, slot_ref, y_ref, lnf_ref, o_ref, *, cap, final):
    tc = x_ref.shape[1]
    acc = x_ref[0]
    slot_t = slot_ref[0]
    col = lax.broadcasted_iota(jnp.int32, (tc, cap), 1)
    for e in range(N_EXPERTS):
        onehot = jnp.where(slot_t[:, e:e + 1] == col, 1.0, 0.0).astype(BF16)
        acc = acc + _dot(onehot, y_ref[0, e])
    if final:
        acc = _rms(acc) * lnf_ref[...]
    o_ref[0] = acc


def _combine(x2, slot_t, y, ln_final, cap, tc, final):
    B, S, D = x2.shape
    E = N_EXPERTS
    return pl.pallas_call(
        functools.partial(_combine_kernel, cap=cap, final=final),
        grid=(B, S // tc),
        in_specs=[pl.BlockSpec((1, tc, D), lambda b, i: (b, i, 0)), pl.BlockSpec((1, tc, E), lambda b, i: (b, i, 0)),
                  pl.BlockSpec((1, E, cap, D), lambda b, i: (b, 0, 0, 0)),
                  pl.BlockSpec(ln_final.shape, lambda b, i: (0, 0))],
        out_specs=pl.BlockSpec((1, tc, D), lambda b, i: (b, i, 0)),
        out_shape=jax.ShapeDtypeStruct((B, S, D), F32),
        compiler_params=_cparams(2),
        name="combine",
    )(x2, slot_t, y, ln_final)


def _layer_weights(l, p):
    d_model = p["w_in"].shape[1]
    w_in = p["w_in"][l]
    o1 = GQA_HEADS * GQA_HEAD_DIM
    o2 = o1 + GQA_KV_HEADS * GQA_HEAD_DIM
    o3 = o2 + GQA_KV_HEADS * GQA_HEAD_DIM
    o4 = o3 + MLA_Q_RANK
    o5 = o4 + MLA_KV_RANK
    heads = lambda w, n: w.reshape(w.shape[0], n, w.shape[1] // n)
    flat = lambda w: w.reshape(w.shape[0], -1)
    wq = flat(_lay_gqa(heads(w_in[:, :o1], GQA_HEADS)))
    wk = flat(_lay_gqa(heads(w_in[:, o1:o2], GQA_KV_HEADS)))
    wv = heads(w_in[:, o2:o3], GQA_KV_HEADS)
    wv = flat(jnp.stack([_lay_v(wv, False), _lay_v(wv, True)], axis=2))
    w_kr = w_in[:, o5:]
    wkr = _lay_mla(jnp.zeros((d_model, MLA_NOPE_DIM), F32), w_kr)
    wcat = jnp.concatenate([wq, wk, wv, w_in[:, o3:o4], w_in[:, o4:o5], wkr], axis=1).astype(BF16)

    wqb = heads(p["w_q_b"][l], MLA_HEADS)
    wqb = flat(_lay_mla(wqb[..., :MLA_NOPE_DIM], wqb[..., MLA_NOPE_DIM:])).astype(BF16)
    wkvb = heads(p["w_kv_b"][l], MLA_HEADS)
    k_nope = wkvb[..., :MLA_NOPE_DIM]
    wkb = flat(_lay_mla(k_nope, jnp.zeros(k_nope.shape[:-1] + (MLA_ROPE_DIM,), F32))).astype(BF16)
    v_part = wkvb[..., MLA_NOPE_DIM:].reshape(MLA_KV_RANK, MLA_HEADS // 2, 2, MLA_V_DIM)
    wvb = flat(jnp.stack([_lay_v(v_part[:, :, 0], False), _lay_v(v_part[:, :, 1], True)], axis=2)).astype(BF16)

    q_scale = GQA_HEAD_DIM ** -0.5 * LOG2E
    gq = _lay_gqa(p["gqa_q_norm"][l] * q_scale).reshape(1, LANES)
    gk = _lay_gqa(p["gqa_k_norm"][l]).reshape(1, LANES)
    row = lambda v: v.reshape(1, -1)
    return {
        "ln_mix": row(p["ln_mix"][l]), "wcat": wcat, "wqb": wqb, "wkb": wkb, "wvb": wvb, "gq": gq, "gk": gk,
        "gql": row(p["mla_q_norm"][l]), "gkv": row(p["mla_kv_norm"][l]),
        "on": jnp.concatenate([p["out_norm_gqa"][l], p["out_norm_mla"][l]]).reshape(1, -1),
        "w_o": p["w_o"][l].astype(BF16), "ln_mem": row(p["ln_mem"][l]), "ln_mem_kv": row(p["ln_mem_kv"][l]),
        "w_mem_q": p["w_mem_q"][l].astype(BF16), "w_mem_kv": p["w_mem_kv"][l].astype(BF16),
        "w_mem_o": p["w_mem_o"][l].astype(BF16), "ln_ffn": row(p["ln_ffn"][l]),
        "w_router_t": p["w_router"][l].T,
        "w_gate": p["w_gate"][l].astype(BF16), "w_up": p["w_up"][l].astype(BF16),
        "w_down": p["w_down"][l].astype(BF16),
    }


def _tables(seq_len):
    cos_g, sin_g, cos_m, sin_m = _rope_tables(seq_len)
    mq_scale = MLA_QK_DIM ** -0.5 * LOG2E
    one_g = jnp.zeros((GQA_KV_HEADS, 2, LANES), F32).at[:, 0, _ONE_LANE_A].set(1.0).at[:, 1, _ONE_LANE_B].set(1.0)
    one_m = jnp.zeros((MLA_HEADS // 2, 2, LANES), F32).at[:, 0, _ONE_LANE_A].set(1.0).at[:, 1, _ONE_LANE_B].set(1.0)
    return {"cos_g": cos_g, "sin_g": sin_g, "cos_m": cos_m, "sin_m": sin_m,
            "cos_mq": cos_m * mq_scale, "sin_mq": sin_m * mq_scale,
            "one_g": one_g.reshape(1, -1), "one_m": one_m.reshape(1, -1)}


def _pick(n, pref):
    t = min(n, pref)
    assert n % t == 0, (n, t)
    return t


def kernel(x, mem, ln_mix, w_in, gqa_q_norm, gqa_k_norm, mla_q_norm, mla_kv_norm, w_q_b, w_kv_b, out_norm_gqa,
           out_norm_mla, w_o, ln_mem, ln_mem_kv, w_mem_q, w_mem_kv, w_mem_o, ln_ffn, w_router, w_gate, w_up,
           w_down, ln_final):
    p = dict(ln_mix=ln_mix, w_in=w_in, gqa_q_norm=gqa_q_norm, gqa_k_norm=gqa_k_norm, mla_q_norm=mla_q_norm,
             mla_kv_norm=mla_kv_norm, w_q_b=w_q_b, w_kv_b=w_kv_b, out_norm_gqa=out_norm_gqa,
             out_norm_mla=out_norm_mla, w_o=w_o, ln_mem=ln_mem, ln_mem_kv=ln_mem_kv, w_mem_q=w_mem_q,
             w_mem_kv=w_mem_kv, w_mem_o=w_mem_o, ln_ffn=ln_ffn, w_router=w_router, w_gate=w_gate, w_up=w_up,
             w_down=w_down)
    B, S, D = x.shape
    depth = w_in.shape[0]
    E = N_EXPERTS
    assert S % LANES == 0 and S % GRID_W == 0
    cap = EC_CAPACITY_FACTOR * S // E
    chunks = S // LANES
    tabs = _tables(S)
    ln_final2 = ln_final.reshape(1, -1)
    ts_in, ts_post, tq, tk, tc = _pick(S, 256), _pick(S, 512), _pick(S, 512), _pick(S, 512), _pick(S, 256)

    for l in range(depth):
        lw = _layer_weights(l, p)
        q_all, k_all, v_all = _mixer_in(x, lw, tabs, ts_in)
        o = _attention(q_all, k_all, v_all, tq, tk)
        kv_mem = _mem_kv(mem, lw["ln_mem_kv"], lw["w_mem_kv"])
        x2, h3, aff = _post_attn(o, x, kv_mem, lw, ts_post)
        slot = _select(aff.reshape(B, E * chunks, LANES), cap, chunks).reshape(B, E, S)
        y = _experts(slot.reshape(B, E, 1, S), aff.reshape(B, E, 1, S), h3, lw["w_gate"], lw["w_up"], lw["w_down"], cap)
        x = _combine(x2, jnp.swapaxes(slot, 1, 2), y, ln_final2, cap, tc, final=(l == depth - 1))
    return x
```

```python
import functools
import math

import numpy as np
import jax
import jax.numpy as jnp
from jax import lax
from jax.experimental import pallas as pl
from jax.experimental.pallas import tpu as pltpu

F32 = jnp.float32
BF16 = jnp.bfloat16

GRID_W = 64
ROPE_THETA = 10000.0
EPS = 1e-6
GQA_HEADS = 8
GQA_KV_HEADS = 2
GQA_GROUP = GQA_HEADS // GQA_KV_HEADS
GQA_HEAD_DIM = 64
MLA_HEADS = 8
MLA_Q_RANK = 256
MLA_KV_RANK = 128
MLA_NOPE_DIM = 64
MLA_ROPE_DIM = 32
MLA_V_DIM = 64
MLA_QK_DIM = MLA_NOPE_DIM + MLA_ROPE_DIM
MEM_HEADS = 4
MEM_HEAD_DIM = 128
N_EXPERTS = 16
EC_CAPACITY_FACTOR = 2

LANES = 128
LOG2E = math.log2(math.e)
VMEM_LIMIT = 56 * 1024 * 1024

N_HEADS = GQA_HEADS + MLA_HEADS
N_PAIRS = N_HEADS // 2
GQA_PAIRS = GQA_HEADS // 2


def _cparams(n_axes):
    return pltpu.CompilerParams(dimension_semantics=("arbitrary",) * n_axes, vmem_limit_bytes=VMEM_LIMIT)


def _rms(x, eps=EPS):
    return x * lax.rsqrt(jnp.mean(x * x, axis=-1, keepdims=True) + eps)


def _dot(a, b):
    return jnp.dot(a, b, preferred_element_type=F32)


def _dot_nt(a, b):
    return lax.dot_general(a, b, (((1,), (1,)), ((), ())), preferred_element_type=F32)


_ONE_LANE_A = 64
_ONE_LANE_B = 0


def _zeros_like_cols(w, n):
    return jnp.zeros(w.shape[:-1] + (n,), w.dtype)


def _lay_gqa(w):
    z = _zeros_like_cols(w, 32)
    return jnp.concatenate([w[..., :32], z, w[..., 32:], z], axis=-1)


def _lay_mla(nope, rope):
    z = _zeros_like_cols(nope, 16)
    return jnp.concatenate([nope[..., :32], rope[..., :16], z, nope[..., 32:], rope[..., 16:], z], axis=-1)


def _lay_v(v, second):
    z = _zeros_like_cols(v, 64)
    return jnp.concatenate([z, v] if second else [v, z], axis=-1)


def _rope_tables(seq_len):
    rows = seq_len // GRID_W
    row = jnp.repeat(jnp.arange(rows, dtype=F32), GRID_W)
    col = jnp.tile(jnp.arange(GRID_W, dtype=F32), rows)

    def angles(rot_dim):
        axis_dim = rot_dim // 2
        inv_freq = ROPE_THETA ** (-jnp.arange(0, axis_dim, 2, dtype=F32) / axis_dim)
        ang = jnp.concatenate([row[:, None] * inv_freq[None, :], col[:, None] * inv_freq[None, :]], axis=-1)
        return jnp.cos(ang), jnp.sin(ang)

    cg, sg = angles(GQA_HEAD_DIM)
    cm, sm = angles(MLA_ROPE_DIM)
    cos_g = _lay_gqa(jnp.concatenate([cg, cg], axis=-1))
    sin_g = _lay_gqa(jnp.concatenate([-sg, sg], axis=-1))
    one = jnp.ones((seq_len, MLA_NOPE_DIM), F32)
    cos_m = _lay_mla(one, jnp.concatenate([cm, cm], axis=-1))
    sin_m = _lay_mla(0.0 * one, jnp.concatenate([-sm, sm], axis=-1))
    return cos_g, sin_g, cos_m, sin_m


_W_SPLITS = (GQA_HEADS * LANES, GQA_KV_HEADS * LANES, 2 * GQA_KV_HEADS * LANES, MLA_Q_RANK, MLA_KV_RANK, LANES)
_W_OFFS = tuple(int(v) for v in np.cumsum((0,) + _W_SPLITS))


def _mixer_in_kernel(x_ref, ln_ref, wcat_ref, wqb_ref, wkb_ref, wvb_ref, gq_ref, gk_ref, gql_ref, gkv_ref,
                     cg_ref, sg_ref, cmq_ref, smq_ref, cmk_ref, smk_ref, oneg_ref, onem_ref,
                     q_ref, k_ref, v_ref):
    x = x_ref[0]
    h = (_rms(x) * ln_ref[...]).astype(BF16)
    proj = _dot(h, wcat_ref[...])
    o = _W_OFFS
    cg, sg = cg_ref[...], sg_ref[...]

    def head_norm_rope(blk, gain):
        ss = jnp.sum(blk * blk, axis=-1, keepdims=True) * (1.0 / GQA_HEAD_DIM)
        y = blk * lax.rsqrt(ss + EPS) * gain
        return y * cg + pltpu.roll(y, 64, 1) * sg

    for j in range(GQA_HEADS):
        blk = proj[:, o[0] + j * LANES:o[0] + (j + 1) * LANES]
        q_ref[0, :, j * LANES:(j + 1) * LANES] = head_norm_rope(blk, gq_ref[...]).astype(BF16)
    for j in range(GQA_KV_HEADS):
        blk = proj[:, o[1] + j * LANES:o[1] + (j + 1) * LANES]
        k_ref[0, :, j * LANES:(j + 1) * LANES] = head_norm_rope(blk, gk_ref[...]).astype(BF16)
    n_vg = 2 * GQA_KV_HEADS * LANES
    v_ref[0, :, 0:n_vg] = (proj[:, o[2]:o[3]] + oneg_ref[...]).astype(BF16)

    c_q = (_rms(proj[:, o[3]:o[4]]) * gql_ref[...]).astype(BF16)
    qm = _dot(c_q, wqb_ref[...])
    cmq, smq = cmq_ref[...], smq_ref[...]
    for j in range(MLA_HEADS):
        blk = qm[:, j * LANES:(j + 1) * LANES]
        q_ref[0, :, (GQA_HEADS + j) * LANES:(GQA_HEADS + j + 1) * LANES] = (
            blk * cmq + pltpu.roll(blk, 64, 1) * smq).astype(BF16)

    c_kv = (_rms(proj[:, o[4]:o[5]]) * gkv_ref[...]).astype(BF16)
    kn = _dot(c_kv, wkb_ref[...])
    vm = _dot(c_kv, wvb_ref[...]) + onem_ref[...]
    kr = proj[:, o[5]:o[6]]
    kr = kr * cmk_ref[...] + pltpu.roll(kr, 64, 1) * smk_ref[...]
    for j in range(MLA_HEADS):
        k_ref[0, :, (GQA_KV_HEADS + j) * LANES:(GQA_KV_HEADS + j + 1) * LANES] = (
            kn[:, j * LANES:(j + 1) * LANES] + kr).astype(BF16)
    v_ref[0, :, n_vg:] = vm.astype(BF16)


def _mixer_in(x, lw, tabs, ts):
    B, S, D = x.shape
    nq, nk, nv = N_HEADS * LANES, (GQA_KV_HEADS + MLA_HEADS) * LANES, (2 * GQA_KV_HEADS + MLA_HEADS) * LANES
    full = lambda a: pl.BlockSpec(a.shape, lambda b, i: (0,) * a.ndim)
    tab = pl.BlockSpec((ts, LANES), lambda b, i: (i, 0))
    consts = (lw["ln_mix"], lw["wcat"], lw["wqb"], lw["wkb"], lw["wvb"], lw["gq"], lw["gk"], lw["gql"], lw["gkv"])
    return pl.pallas_call(
        _mixer_in_kernel,
        grid=(B, S // ts),
        in_specs=[pl.BlockSpec((1, ts, D), lambda b, i: (b, i, 0))] + [full(a) for a in consts]
        + [tab] * 6 + [full(tabs["one_g"]), full(tabs["one_m"])],
        out_specs=[pl.BlockSpec((1, ts, nq), lambda b, i: (b, i, 0)),
                   pl.BlockSpec((1, ts, nk), lambda b, i: (b, i, 0)),
                   pl.BlockSpec((1, ts, nv), lambda b, i: (b, i, 0))],
        out_shape=[jax.ShapeDtypeStruct((B, S, nq), BF16), jax.ShapeDtypeStruct((B, S, nk), BF16),
                   jax.ShapeDtypeStruct((B, S, nv), BF16)],
        compiler_params=_cparams(2),
        name="mixer_in",
    )(x, *consts, tabs["cos_g"], tabs["sin_g"], tabs["cos_mq"], tabs["sin_mq"], tabs["cos_m"], tabs["sin_m"],
      tabs["one_g"], tabs["one_m"])


def _attn_kernel(q_ref, ka_ref, kb_ref, va_ref, vb_ref, o_ref, *, tk):
    tq = q_ref.shape[1]
    n_chunks = ka_ref.shape[1] // tk

    def chunk(q, k_ref, v_ref, c, m, acc):
        ks = k_ref[0, c * tk:(c + 1) * tk, :]
        vs = v_ref[0, c * tk:(c + 1) * tk, :]
        s = _dot_nt(q, ks)
        m_new = jnp.maximum(m, jnp.max(s, axis=-1, keepdims=True))
        alpha = jnp.exp2(m - m_new)
        p = jnp.exp2(s - m_new).astype(BF16)
        return m_new, alpha * acc + _dot(p, vs)

    q_a = q_ref[0, :, 0:LANES]
    q_b = q_ref[0, :, LANES:2 * LANES]
    m_a = m_b = jnp.full((tq, 1), -jnp.inf, F32)
    acc_a = acc_b = jnp.zeros((tq, LANES), F32)
    for c in range(n_chunks):
        m_a, acc_a = chunk(q_a, ka_ref, va_ref, c, m_a, acc_a)
        m_b, acc_b = chunk(q_b, kb_ref, vb_ref, c, m_b, acc_b)
    out_a = acc_a * (1.0 / acc_a[:, _ONE_LANE_A:_ONE_LANE_A + 1])
    out_b = acc_b * (1.0 / acc_b[:, _ONE_LANE_B:_ONE_LANE_B + 1])
    lane = lax.broadcasted_iota(jnp.int32, (tq, LANES), 1)
    o_ref[0] = jnp.where(lane < 64, out_a, out_b)


def _attention(q_all, k_all, v_all, tq, tk):
    B, S, _ = q_all.shape
    gp = GQA_PAIRS
    pairs_per_kv = GQA_GROUP // 2

    def k_col(p, second):
        return jnp.where(p < gp, p // pairs_per_kv, GQA_KV_HEADS + 2 * (p - gp) + second)

    def v_col(p, second):
        return jnp.where(p < gp, 2 * (p // pairs_per_kv) + second, 2 * GQA_KV_HEADS + 2 * (p - gp) + second)

    kv_spec = lambda col, second: pl.BlockSpec((1, S, LANES), lambda b, p, i: (b, 0, col(p, second)))
    return pl.pallas_call(
        functools.partial(_attn_kernel, tk=tk),
        grid=(B, N_PAIRS, S // tq),
        in_specs=[pl.BlockSpec((1, tq, 2 * LANES), lambda b, p, i: (b, i, p)),
                  kv_spec(k_col, 0), kv_spec(k_col, 1), kv_spec(v_col, 0), kv_spec(v_col, 1)],
        out_specs=pl.BlockSpec((1, tq, LANES), lambda b, p, i: (b, i, p)),
        out_shape=jax.ShapeDtypeStruct((B, S, N_PAIRS * LANES), F32),
        compiler_params=_cparams(3),
        name="attention",
    )(q_all, k_all, k_all, v_all, v_all)


def _mem_kv_kernel(m_ref, ln_ref, w_ref, o_ref):
    h = (_rms(m_ref[0]) * ln_ref[...]).astype(BF16)
    o_ref[0] = _dot(h, w_ref[...]).astype(BF16)


def _mem_kv(mem, ln, w):
    B, M, D = mem.shape
    n = w.shape[1]
    return pl.pallas_call(
        _mem_kv_kernel,
        grid=(B,),
        in_specs=[pl.BlockSpec((1, M, D), lambda b: (b, 0, 0)), pl.BlockSpec(ln.shape, lambda b: (0, 0)),
                  pl.BlockSpec(w.shape, lambda b: (0, 0))],
        out_specs=pl.BlockSpec((1, M, n), lambda b: (b, 0, 0)),
        out_shape=jax.ShapeDtypeStruct((B, M, n), BF16),
        compiler_params=_cparams(1),
        name="mem_kv",
    )(mem, ln, w)


def _post_attn_kernel(o_ref, x_ref, on_ref, wo_ref, lnm_ref, wmq_ref, kv_ref, wmo_ref, lnf_ref, wr_ref,
                      x2_ref, h3_ref, aff_ref):
    o = o_ref[0]
    half = o.shape[1] // 2
    merged = (jnp.concatenate([_rms(o[:, :half]), _rms(o[:, half:])], axis=-1) * on_ref[...]).astype(BF16)
    x1 = x_ref[0] + _dot(merged, wo_ref[...])

    h2 = (_rms(x1) * lnm_ref[...]).astype(BF16)
    q = (_dot(h2, wmq_ref[...]) * (MEM_HEAD_DIM ** -0.5 * LOG2E)).astype(BF16)
    kv = kv_ref[0]
    n_mem = MEM_HEADS * MEM_HEAD_DIM
    outs = []
    for hh in range(MEM_HEADS):
        lo, hi = hh * MEM_HEAD_DIM, (hh + 1) * MEM_HEAD_DIM
        s = _dot_nt(q[:, lo:hi], kv[:, lo:hi])
        p = jnp.exp2(s - jnp.max(s, axis=-1, keepdims=True))
        l = jnp.sum(p, axis=-1, keepdims=True)
        outs.append(_dot(p.astype(BF16), kv[:, n_mem + lo:n_mem + hi]) * (1.0 / l))
    oc = jnp.concatenate(outs, axis=-1).astype(BF16)
    x2 = x1 + _dot(oc, wmo_ref[...])
    x2_ref[0] = x2

    h3 = _rms(x2) * lnf_ref[...]
    h3_ref[0] = h3.astype(BF16)
    logits = lax.dot_general(wr_ref[...], h3, (((1,), (1,)), ((), ())), preferred_element_type=F32,
                             precision=lax.Precision.HIGHEST)
    e = jnp.exp(logits - jnp.max(logits, axis=0, keepdims=True))
    aff_ref[0] = e * (1.0 / jnp.sum(e, axis=0, keepdims=True))


def _post_attn(o, x, kv_mem, lw, ts):
    B, S, D = x.shape
    E = N_EXPERTS
    consts_a = (lw["on"], lw["w_o"], lw["ln_mem"], lw["w_mem_q"])
    consts_b = (lw["w_mem_o"], lw["ln_ffn"], lw["w_router_t"])
    full = lambda a: pl.BlockSpec(a.shape, lambda b, i: (0,) * a.ndim)
    tile = lambda n: pl.BlockSpec((1, ts, n), lambda b, i: (b, i, 0))
    return pl.pallas_call(
        _post_attn_kernel,
        grid=(B, S // ts),
        in_specs=[tile(o.shape[2]), tile(D)] + [full(a) for a in consts_a]
        + [pl.BlockSpec((1,) + kv_mem.shape[1:], lambda b, i: (b, 0, 0))] + [full(a) for a in consts_b],
        out_specs=[tile(D), tile(D), pl.BlockSpec((1, E, ts), lambda b, i: (b, 0, i))],
        out_shape=[jax.ShapeDtypeStruct((B, S, D), F32), jax.ShapeDtypeStruct((B, S, D), BF16),
                   jax.ShapeDtypeStruct((B, E, S), F32)],
        compiler_params=_cparams(2),
        name="post_attn",
    )(o, x, *consts_a, kv_mem, *consts_b)


def _select_kernel(aff_ref, slot_ref, *, cap, chunks):
    a = aff_ref[0]
    rows = a.shape[0]
    bits = pltpu.bitcast(a, jnp.int32)

    r_i = lax.broadcasted_iota(jnp.int32, (rows, rows), 0)
    c_i = lax.broadcasted_iota(jnp.int32, (rows, rows), 1)
    same = (r_i // chunks) == (c_i // chunks)
    bd_all = jnp.where(same, 1.0, 0.0).astype(BF16)
    bd_before = jnp.where(same & (c_i < r_i), 1.0, 0.0).astype(BF16)
    l_r = lax.broadcasted_iota(jnp.int32, (LANES, LANES), 0)
    l_c = lax.broadcasted_iota(jnp.int32, (LANES, LANES), 1)
    ones = jnp.ones((LANES, LANES), BF16)
    before = jnp.where(l_r < l_c, 1.0, 0.0).astype(BF16)

    def as01(mask):
        return jnp.where(mask, 1.0, 0.0).astype(BF16)

    def expert_count(x01):
        return _dot(bd_all, _dot(x01, ones).astype(BF16))

    def prefix_excl(x01):
        return _dot(x01, before) + _dot(bd_before, _dot(x01, ones).astype(BF16))

    def step(i, theta):
        cand = theta | (jnp.int32(1) << (30 - i))
        cnt = expert_count(as01(bits >= cand))
        return jnp.where(cnt >= cap, cand, theta)

    theta = lax.fori_loop(0, 31, step, jnp.zeros(bits.shape, jnp.int32))
    gt = bits > theta
    eq = bits == theta
    need = cap - expert_count(as01(gt))
    sel = gt | (eq & (prefix_excl(as01(eq)) < need))
    pos = prefix_excl(as01(sel))
    slot_ref[0] = jnp.where(sel, pos, -1.0).astype(jnp.int32)


def _select(aff2, cap, chunks):
    B, rows, _ = aff2.shape
    return pl.pallas_call(
        functools.partial(_select_kernel, cap=cap, chunks=chunks),
        grid=(B,),
        in_specs=[pl.BlockSpec((1, rows, LANES), lambda b: (b, 0, 0))],
        out_specs=pl.BlockSpec((1, rows, LANES), lambda b: (b, 0, 0)),
        out_shape=jax.ShapeDtypeStruct((B, rows, LANES), jnp.int32),
        compiler_params=_cparams(1),
        name="select",
    )(aff2)


def _expert_kernel(slot_ref, aff_ref, h_ref, wg_ref, wu_ref, wd_ref, y_ref, *, cap):
    slot = slot_ref[0, 0]
    S = slot.shape[1]
    hit = lax.broadcasted_iota(jnp.int32, (cap, S), 0) == slot
    gate = jnp.sum(jnp.where(hit, aff_ref[0, 0], 0.0), axis=-1, keepdims=True)
    onehot = jnp.where(hit, 1.0, 0.0).astype(BF16)
    x_in = _dot(onehot, h_ref[0]).astype(BF16)
    a = _dot(x_in, wg_ref[0])
    u = _dot(x_in, wu_ref[0])
    hm = (a * (1.0 / (1.0 + jnp.exp(-a))) * u).astype(BF16)
    y_ref[0, 0] = (_dot(hm, wd_ref[0]) * gate).astype(BF16)


def _experts(slot_row, aff_row, h3, wg, wu, wd, cap):
    B, S, D = h3.shape
    E = N_EXPERTS
    F = wg.shape[2]
    row = pl.BlockSpec((1, 1, 1, S), lambda b, e: (b, e, 0, 0))
    return pl.pallas_call(
        functools.partial(_expert_kernel, cap=cap),
        grid=(B, E),
        in_specs=[row, row, pl.BlockSpec((1, S, D), lambda b, e: (b, 0, 0)),
                  pl.BlockSpec((1, D, F), lambda b, e: (e, 0, 0)), pl.BlockSpec((1, D, F), lambda b, e: (e, 0, 0)),
                  pl.BlockSpec((1, F, D), lambda b, e: (e, 0, 0))],
        out_specs=pl.BlockSpec((1, 1, cap, D), lambda b, e: (b, e, 0, 0)),
        out_shape=jax.ShapeDtypeStruct((B, E, cap, D), BF16),
        compiler_params=_cparams(2),
        name="experts",
    )(slot_row, aff_row, h3, wg, wu, wd)


def _combine_kernel(x_ref, slot_ref, y_ref, lnf_ref, o_ref, *, cap, final):
    tc = x_ref.shape[1]
    acc = x_ref[0]
    slot_t = slot_ref[0]
    col = lax.broadcasted_iota(jnp.int32, (tc, cap), 1)
    for e in range(N_EXPERTS):
        onehot = jnp.where(slot_t[:, e:e + 1] == col, 1.0, 0.0).astype(BF16)
        acc = acc + _dot(onehot, y_ref[0, e])
    if final:
        acc = _rms(acc) * lnf_ref[...]
    o_ref[0] = acc


def _combine(x2, slot_t, y, ln_final, cap, tc, final):
    B, S, D = x2.shape
    E = N_EXPERTS
    return pl.pallas_call(
        functools.partial(_combine_kernel, cap=cap, final=final),
        grid=(B, S // tc),
        in_specs=[pl.BlockSpec((1, tc, D), lambda b, i: (b, i, 0)), pl.BlockSpec((1, tc, E), lambda b, i: (b, i, 0)),
                  pl.BlockSpec((1, E, cap, D), lambda b, i: (b, 0, 0, 0)),
                  pl.BlockSpec(ln_final.shape, lambda b, i: (0, 0))],
        out_specs=pl.BlockSpec((1, tc, D), lambda b, i: (b, i, 0)),
        out_shape=jax.ShapeDtypeStruct((B, S, D), F32),
        compiler_params=_cparams(2),
        name="combine",
    )(x2, slot_t, y, ln_final)


def _layer_weights(l, p):
    d_model = p["w_in"].shape[1]
    w_in = p["w_in"][l]
    o1 = GQA_HEADS * GQA_HEAD_DIM
    o2 = o1 + GQA_KV_HEADS * GQA_HEAD_DIM
    o3 = o2 + GQA_KV_HEADS * GQA_HEAD_DIM
    o4 = o3 + MLA_Q_RANK
    o5 = o4 + MLA_KV_RANK
    heads = lambda w, n: w.reshape(w.shape[0], n, w.shape[1] // n)
    flat = lambda w: w.reshape(w.shape[0], -1)
    wq = flat(_lay_gqa(heads(w_in[:, :o1], GQA_HEADS)))
    wk = flat(_lay_gqa(heads(w_in[:, o1:o2], GQA_KV_HEADS)))
    wv = heads(w_in[:, o2:o3], GQA_KV_HEADS)
    wv = flat(jnp.stack([_lay_v(wv, False), _lay_v(wv, True)], axis=2))
    w_kr = w_in[:, o5:]
    wkr = _lay_mla(jnp.zeros((d_model, MLA_NOPE_DIM), F32), w_kr)
    wcat = jnp.concatenate([wq, wk, wv, w_in[:, o3:o4], w_in[:, o4:o5], wkr], axis=1).astype(BF16)

    wqb = heads(p["w_q_b"][l], MLA_HEADS)
    wqb = flat(_lay_mla(wqb[..., :MLA_NOPE_DIM], wqb[..., MLA_NOPE_DIM:])).astype(BF16)
    wkvb = heads(p["w_kv_b"][l], MLA_HEADS)
    k_nope = wkvb[..., :MLA_NOPE_DIM]
    wkb = flat(_lay_mla(k_nope, jnp.zeros(k_nope.shape[:-1] + (MLA_ROPE_DIM,), F32))).astype(BF16)
    v_part = wkvb[..., MLA_NOPE_DIM:].reshape(MLA_KV_RANK, MLA_HEADS // 2, 2, MLA_V_DIM)
    wvb = flat(jnp.stack([_lay_v(v_part[:, :, 0], False), _lay_v(v_part[:, :, 1], True)], axis=2)).astype(BF16)

    q_scale = GQA_HEAD_DIM ** -0.5 * LOG2E
    gq = _lay_gqa(p["gqa_q_norm"][l] * q_scale).reshape(1, LANES)
    gk = _lay_gqa(p["gqa_k_norm"][l]).reshape(1, LANES)
    row = lambda v: v.reshape(1, -1)
    return {
        "ln_mix": row(p["ln_mix"][l]), "wcat": wcat, "wqb": wqb, "wkb": wkb, "wvb": wvb, "gq": gq, "gk": gk,
        "gql": row(p["mla_q_norm"][l]), "gkv": row(p["mla_kv_norm"][l]),
        "on": jnp.concatenate([p["out_norm_gqa"][l], p["out_norm_mla"][l]]).reshape(1, -1),
        "w_o": p["w_o"][l].astype(BF16), "ln_mem": row(p["ln_mem"][l]), "ln_mem_kv": row(p["ln_mem_kv"][l]),
        "w_mem_q": p["w_mem_q"][l].astype(BF16), "w_mem_kv": p["w_mem_kv"][l].astype(BF16),
        "w_mem_o": p["w_mem_o"][l].astype(BF16), "ln_ffn": row(p["ln_ffn"][l]),
        "w_router_t": p["w_router"][l].T,
        "w_gate": p["w_gate"][l].astype(BF16), "w_up": p["w_up"][l].astype(BF16),
        "w_down": p["w_down"][l].astype(BF16),
    }


def _tables(seq_len):
    cos_g, sin_g, cos_m, sin_m = _rope_tables(seq_len)
    mq_scale = MLA_QK_DIM ** -0.5 * LOG2E
    one_g = jnp.zeros((GQA_KV_HEADS, 2, LANES), F32).at[:, 0, _ONE_LANE_A].set(1.0).at[:, 1, _ONE_LANE_B].set(1.0)
    one_m = jnp.zeros((MLA_HEADS // 2, 2, LANES), F32).at[:, 0, _ONE_LANE_A].set(1.0).at[:, 1, _ONE_LANE_B].set(1.0)
    return {"cos_g": cos_g, "sin_g": sin_g, "cos_m": cos_m, "sin_m": sin_m,
            "cos_mq": cos_m * mq_scale, "sin_mq": sin_m * mq_scale,
            "one_g": one_g.reshape(1, -1), "one_m": one_m.reshape(1, -1)}


def _pick(n, pref):
    t = min(n, pref)
    assert n % t == 0, (n, t)
    return t


def kernel(x, mem, ln_mix, w_in, gqa_q_norm, gqa_k_norm, mla_q_norm, mla_kv_norm, w_q_b, w_kv_b, out_norm_gqa,
           out_norm_mla, w_o, ln_mem, ln_mem_kv, w_mem_q, w_mem_kv, w_mem_o, ln_ffn, w_router, w_gate, w_up,
           w_down, ln_final):
    p = dict(ln_mix=ln_mix, w_in=w_in, gqa_q_norm=gqa_q_norm, gqa_k_norm=gqa_k_norm, mla_q_norm=mla_q_norm,
             mla_kv_norm=mla_kv_norm, w_q_b=w_q_b, w_kv_b=w_kv_b, out_norm_gqa=out_norm_gqa,
             out_norm_mla=out_norm_mla, w_o=w_o, ln_mem=ln_mem, ln_mem_kv=ln_mem_kv, w_mem_q=w_mem_q,
             w_mem_kv=w_mem_kv, w_mem_o=w_mem_o, ln_ffn=ln_ffn, w_router=w_router, w_gate=w_gate, w_up=w_up,
             w_down=w_down)
    B, S, D = x.shape
    depth = w_in.shape[0]
    E = N_EXPERTS
    assert S % LANES == 0 and S % GRID_W == 0
    cap = EC_CAPACITY_FACTOR * S // E
    chunks = S // LANES
    tabs = _tables(S)
    ln_final2 = ln_final.reshape(1, -1)
    ts_in, ts_post, tq, tk, tc = _pick(S, 256), _pick(S, 512), _pick(S, 512), _pick(S, 512), _pick(S, 256)

    for l in range(depth):
        lw = _layer_weights(l, p)
        q_all, k_all, v_all = _mixer_in(x, lw, tabs, ts_in)
        o = _attention(q_all, k_all, v_all, tq, tk)
        kv_mem = _mem_kv(mem, lw["ln_mem_kv"], lw["w_mem_kv"])
        x2, h3, aff = _post_attn(o, x, kv_mem, lw, ts_post)
        slot = _select(aff.reshape(B, E * chunks, LANES), cap, chunks).reshape(B, E, S)
        y = _experts(slot.reshape(B, E, 1, S), aff.reshape(B, E, 1, S), h3, lw["w_gate"], lw["w_up"], lw["w_down"], cap)
        x = _combine(x2, jnp.swapaxes(slot, 1, 2), y, ln_final2, cap, tc, final=(l == depth - 1))
    return x
```

```python
import functools
import math

import numpy as np
import jax
import jax.numpy as jnp
from jax import lax
from jax.experimental import pallas as pl
from jax.experimental.pallas import tpu as pltpu

F32 = jnp.float32
BF16 = jnp.bfloat16

GRID_W = 64
ROPE_THETA = 10000.0
EPS = 1e-6
GQA_HEADS = 8
GQA_KV_HEADS = 2
GQA_GROUP = GQA_HEADS // GQA_KV_HEADS
GQA_HEAD_DIM = 64
MLA_HEADS = 8
MLA_Q_RANK = 256
MLA_KV_RANK = 128
MLA_NOPE_DIM = 64
MLA_ROPE_DIM = 32
MLA_V_DIM = 64
MLA_QK_DIM = MLA_NOPE_DIM + MLA_ROPE_DIM
MEM_HEADS = 4
MEM_HEAD_DIM = 128
N_EXPERTS = 16
EC_CAPACITY_FACTOR = 2

LANES = 128
LOG2E = math.log2(math.e)
VMEM_LIMIT = 56 * 1024 * 1024

N_HEADS = GQA_HEADS + MLA_HEADS
N_PAIRS = N_HEADS // 2
GQA_PAIRS = GQA_HEADS // 2


def _cparams(n_axes):
    return pltpu.CompilerParams(dimension_semantics=("arbitrary",) * n_axes, vmem_limit_bytes=VMEM_LIMIT)


def _rms(x, eps=EPS):
    return x * lax.rsqrt(jnp.mean(x * x, axis=-1, keepdims=True) + eps)


def _dot(a, b):
    return jnp.dot(a, b, preferred_element_type=F32)


def _dot_nt(a, b):
    return lax.dot_general(a, b, (((1,), (1,)), ((), ())), preferred_element_type=F32)


V_DIM = 64
ONE_LANE = V_DIM


def _zeros_like_cols(w, n):
    return jnp.zeros(w.shape[:-1] + (n,), w.dtype)


def _lay_gqa(w):
    z = _zeros_like_cols(w, 32)
    return jnp.concatenate([w[..., :32], z, w[..., 32:], z], axis=-1)


def _lay_mla(nope, rope):
    z = _zeros_like_cols(nope, 16)
    return jnp.concatenate([nope[..., :32], rope[..., :16], z, nope[..., 32:], rope[..., 16:], z], axis=-1)


def _lay_v(v):
    return jnp.concatenate([v, _zeros_like_cols(v, LANES - V_DIM)], axis=-1)


def _rope_tables(seq_len):
    rows = seq_len // GRID_W
    row = jnp.repeat(jnp.arange(rows, dtype=F32), GRID_W)
    col = jnp.tile(jnp.arange(GRID_W, dtype=F32), rows)

    def angles(rot_dim):
        axis_dim = rot_dim // 2
        inv_freq = ROPE_THETA ** (-jnp.arange(0, axis_dim, 2, dtype=F32) / axis_dim)
        ang = jnp.concatenate([row[:, None] * inv_freq[None, :], col[:, None] * inv_freq[None, :]], axis=-1)
        return jnp.cos(ang), jnp.sin(ang)

    cg, sg = angles(GQA_HEAD_DIM)
    cm, sm = angles(MLA_ROPE_DIM)
    cos_g = _lay_gqa(jnp.concatenate([cg, cg], axis=-1))
    sin_g = _lay_gqa(jnp.concatenate([-sg, sg], axis=-1))
    one = jnp.ones((seq_len, MLA_NOPE_DIM), F32)
    cos_m = _lay_mla(one, jnp.concatenate([cm, cm], axis=-1))
    sin_m = _lay_mla(0.0 * one, jnp.concatenate([-sm, sm], axis=-1))
    return cos_g, sin_g, cos_m, sin_m


_W_SPLITS = (GQA_HEADS * LANES, GQA_KV_HEADS * LANES, GQA_KV_HEADS * LANES, MLA_Q_RANK, MLA_KV_RANK, LANES)
_W_OFFS = tuple(int(v) for v in np.cumsum((0,) + _W_SPLITS))


def _mixer_in_kernel(x_ref, ln_ref, wcat_ref, wqb_ref, wkb_ref, wvb_ref, gq_ref, gk_ref, gql_ref, gkv_ref,
                     cg_ref, sg_ref, cmq_ref, smq_ref, cmk_ref, smk_ref, oneg_ref, onem_ref,
                     q_ref, k_ref, v_ref):
    x = x_ref[0]
    h = (_rms(x) * ln_ref[...]).astype(BF16)
    proj = _dot(h, wcat_ref[...])
    o = _W_OFFS
    cg, sg = cg_ref[...], sg_ref[...]

    def head_norm_rope(blk, gain):
        ss = jnp.sum(blk * blk, axis=-1, keepdims=True) * (1.0 / GQA_HEAD_DIM)
        y = blk * lax.rsqrt(ss + EPS) * gain
        return y * cg + pltpu.roll(y, 64, 1) * sg

    for j in range(GQA_HEADS):
        blk = proj[:, o[0] + j * LANES:o[0] + (j + 1) * LANES]
        q_ref[0, :, j * LANES:(j + 1) * LANES] = head_norm_rope(blk, gq_ref[...]).astype(BF16)
    for j in range(GQA_KV_HEADS):
        blk = proj[:, o[1] + j * LANES:o[1] + (j + 1) * LANES]
        k_ref[0, :, j * LANES:(j + 1) * LANES] = head_norm_rope(blk, gk_ref[...]).astype(BF16)
    n_vg = GQA_KV_HEADS * LANES
    v_ref[0, :, 0:n_vg] = (proj[:, o[2]:o[3]] + oneg_ref[...]).astype(BF16)

    c_q = (_rms(proj[:, o[3]:o[4]]) * gql_ref[...]).astype(BF16)
    qm = _dot(c_q, wqb_ref[...])
    cmq, smq = cmq_ref[...], smq_ref[...]
    for j in range(MLA_HEADS):
        blk = qm[:, j * LANES:(j + 1) * LANES]
        q_ref[0, :, (GQA_HEADS + j) * LANES:(GQA_HEADS + j + 1) * LANES] = (
            blk * cmq + pltpu.roll(blk, 64, 1) * smq).astype(BF16)

    c_kv = (_rms(proj[:, o[4]:o[5]]) * gkv_ref[...]).astype(BF16)
    kn = _dot(c_kv, wkb_ref[...])
    vm = _dot(c_kv, wvb_ref[...]) + onem_ref[...]
    kr = proj[:, o[5]:o[6]]
    kr = kr * cmk_ref[...] + pltpu.roll(kr, 64, 1) * smk_ref[...]
    for j in range(MLA_HEADS):
        k_ref[0, :, (GQA_KV_HEADS + j) * LANES:(GQA_KV_HEADS + j + 1) * LANES] = (
            kn[:, j * LANES:(j + 1) * LANES] + kr).astype(BF16)
    v_ref[0, :, n_vg:] = vm.astype(BF16)


def _mixer_in(x, lw, tabs, ts):
    B, S, D = x.shape
    nq, nk = N_HEADS * LANES, (GQA_KV_HEADS + MLA_HEADS) * LANES
    nv = nk
    full = lambda a: pl.BlockSpec(a.shape, lambda b, i: (0,) * a.ndim)
    tab = pl.BlockSpec((ts, LANES), lambda b, i: (i, 0))
    consts = (lw["ln_mix"], lw["wcat"], lw["wqb"], lw["wkb"], lw["wvb"], lw["gq"], lw["gk"], lw["gql"], lw["gkv"])
    return pl.pallas_call(
        _mixer_in_kernel,
        grid=(B, S // ts),
        in_specs=[pl.BlockSpec((1, ts, D), lambda b, i: (b, i, 0))] + [full(a) for a in consts]
        + [tab] * 6 + [full(tabs["one_g"]), full(tabs["one_m"])],
        out_specs=[pl.BlockSpec((1, ts, nq), lambda b, i: (b, i, 0)),
                   pl.BlockSpec((1, ts, nk), lambda b, i: (b, i, 0)),
                   pl.BlockSpec((1, ts, nv), lambda b, i: (b, i, 0))],
        out_shape=[jax.ShapeDtypeStruct((B, S, nq), BF16), jax.ShapeDtypeStruct((B, S, nk), BF16),
                   jax.ShapeDtypeStruct((B, S, nv), BF16)],
        compiler_params=_cparams(2),
        name="mixer_in",
    )(x, *consts, tabs["cos_g"], tabs["sin_g"], tabs["cos_mq"], tabs["sin_mq"], tabs["cos_m"], tabs["sin_m"],
      tabs["one_g"], tabs["one_m"])


SCORE_BOUND_MAX = 40.0
BOUND_SLACK = 1.02


def _dot_tn(a, b):
    return lax.dot_general(a, b, (((0,), (0,)), ((), ())), preferred_element_type=F32)


def _attn_kernel(q_ref, ka_ref, kb_ref, va_ref, vb_ref, o_ref, kmax_ref, *, tk):
    tq = q_ref.shape[1]
    n_chunks = ka_ref.shape[1] // tk
    ones8 = jnp.ones((8, LANES), BF16)
    k_refs = (ka_ref, kb_ref)
    v_refs = (va_ref, vb_ref)

    @pl.when(pl.program_id(2) == 0)
    def _():
        for h in range(2):
            kk = k_refs[h][0]
            ksq = _dot_nt(ones8, kk * kk)
            kmax_ref[h] = jnp.broadcast_to(jnp.max(ksq, axis=-1, keepdims=True), (8, LANES))

    qs = (q_ref[0, :, 0:LANES], q_ref[0, :, LANES:2 * LANES])
    bounds = []
    for h in range(2):
        qsq = _dot_nt(ones8, qs[h] * qs[h])[0:1]
        bounds.append(jnp.sqrt(qsq * kmax_ref[h][0:1, 0:1]) * BOUND_SLACK)
    bound_max = jnp.max(jnp.maximum(bounds[0], bounds[1]))

    def finish(accs):
        outs = [a[0:V_DIM] * (1.0 / a[ONE_LANE:ONE_LANE + 1]) for a in accs]
        o_ref[0] = jnp.concatenate(outs, axis=0).T

    def bounded():
        accs = [jnp.zeros((LANES, tq), F32), jnp.zeros((LANES, tq), F32)]
        for c in range(n_chunks):
            for h in range(2):
                ks = k_refs[h][0, c * tk:(c + 1) * tk, :]
                vs = v_refs[h][0, c * tk:(c + 1) * tk, :]
                pt = jnp.exp2(_dot_nt(ks, qs[h]) - bounds[h]).astype(BF16)
                accs[h] = accs[h] + _dot_tn(vs, pt)
        finish(accs)

    def running_max():
        accs = []
        for h in range(2):
            def body(c, carry):
                m, acc = carry
                start = pl.multiple_of(c * tk, tk)
                ks = k_refs[h][0, pl.ds(start, tk), :]
                vs = v_refs[h][0, pl.ds(start, tk), :]
                st = _dot_nt(ks, qs[h])
                m_new = jnp.maximum(m, jnp.max(st, axis=0, keepdims=True))
                pt = jnp.exp2(st - m_new).astype(BF16)
                return m_new, jnp.exp2(m - m_new) * acc + _dot_tn(vs, pt)

            init = (jnp.full((1, tq), -jnp.inf, F32), jnp.zeros((LANES, tq), F32))
            accs.append(lax.fori_loop(0, n_chunks, body, init)[1])
        finish(accs)

    pl.when(bound_max <= SCORE_BOUND_MAX)(bounded)
    pl.when(jnp.logical_not(bound_max <= SCORE_BOUND_MAX))(running_max)


def _attention(q_all, k_all, v_all, tq, tk):
    B, S, _ = q_all.shape
    pairs_per_kv = GQA_GROUP // 2

    def kv_col(p, second):
        return jnp.where(p < GQA_PAIRS, p // pairs_per_kv, GQA_KV_HEADS + 2 * (p - GQA_PAIRS) + second)

    kv_spec = lambda second: pl.BlockSpec((1, S, LANES), lambda b, p, i: (b, 0, kv_col(p, second)))
    return pl.pallas_call(
        functools.partial(_attn_kernel, tk=tk),
        grid=(B, N_PAIRS, S // tq),
        in_specs=[pl.BlockSpec((1, tq, 2 * LANES), lambda b, p, i: (b, i, p)),
                  kv_spec(0), kv_spec(1), kv_spec(0), kv_spec(1)],
        out_specs=pl.BlockSpec((1, tq, LANES), lambda b, p, i: (b, i, p)),
        out_shape=jax.ShapeDtypeStruct((B, S, N_PAIRS * LANES), F32),
        scratch_shapes=[pltpu.VMEM((2, 8, LANES), F32)],
        compiler_params=_cparams(3),
        name="attention",
    )(q_all, k_all, k_all, v_all, v_all)


def _mem_kv_kernel(m_ref, ln_ref, w_ref, o_ref):
    h = (_rms(m_ref[0]) * ln_ref[...]).astype(BF16)
    o_ref[0] = _dot(h, w_ref[...]).astype(BF16)


def _mem_kv(mem, ln, w):
    B, M, D = mem.shape
    n = w.shape[1]
    return pl.pallas_call(
        _mem_kv_kernel,
        grid=(B,),
        in_specs=[pl.BlockSpec((1, M, D), lambda b: (b, 0, 0)), pl.BlockSpec(ln.shape, lambda b: (0, 0)),
                  pl.BlockSpec(w.shape, lambda b: (0, 0))],
        out_specs=pl.BlockSpec((1, M, n), lambda b: (b, 0, 0)),
        out_shape=jax.ShapeDtypeStruct((B, M, n), BF16),
        compiler_params=_cparams(1),
        name="mem_kv",
    )(mem, ln, w)


def _post_attn_kernel(o_ref, x_ref, on_ref, wo_ref, lnm_ref, wmq_ref, kv_ref, wmo_ref, lnf_ref, wr_ref,
                      x2_ref, h3_ref, aff_ref):
    o = o_ref[0]
    half = o.shape[1] // 2
    merged = (jnp.concatenate([_rms(o[:, :half]), _rms(o[:, half:])], axis=-1) * on_ref[...]).astype(BF16)
    x1 = x_ref[0] + _dot(merged, wo_ref[...])

    h2 = (_rms(x1) * lnm_ref[...]).astype(BF16)
    q = (_dot(h2, wmq_ref[...]) * (MEM_HEAD_DIM ** -0.5 * LOG2E)).astype(BF16)
    kv = kv_ref[0]
    n_mem = MEM_HEADS * MEM_HEAD_DIM
    outs = []
    for hh in range(MEM_HEADS):
        lo, hi = hh * MEM_HEAD_DIM, (hh + 1) * MEM_HEAD_DIM
        s = _dot_nt(q[:, lo:hi], kv[:, lo:hi])
        p = jnp.exp2(s - jnp.max(s, axis=-1, keepdims=True))
        l = jnp.sum(p, axis=-1, keepdims=True)
        outs.append(_dot(p.astype(BF16), kv[:, n_mem + lo:n_mem + hi]) * (1.0 / l))
    oc = jnp.concatenate(outs, axis=-1).astype(BF16)
    x2 = x1 + _dot(oc, wmo_ref[...])
    x2_ref[0] = x2

    h3 = _rms(x2) * lnf_ref[...]
    h3_ref[0] = h3.astype(BF16)
    logits = lax.dot_general(wr_ref[...], h3, (((1,), (1,)), ((), ())), preferred_element_type=F32,
                             precision=lax.Precision.HIGHEST)
    e = jnp.exp(logits - jnp.max(logits, axis=0, keepdims=True))
    aff_ref[0] = e * (1.0 / jnp.sum(e, axis=0, keepdims=True))


def _post_attn(o, x, kv_mem, lw, ts):
    B, S, D = x.shape
    E = N_EXPERTS
    consts_a = (lw["on"], lw["w_o"], lw["ln_mem"], lw["w_mem_q"])
    consts_b = (lw["w_mem_o"], lw["ln_ffn"], lw["w_router_t"])
    full = lambda a: pl.BlockSpec(a.shape, lambda b, i: (0,) * a.ndim)
    tile = lambda n: pl.BlockSpec((1, ts, n), lambda b, i: (b, i, 0))
    return pl.pallas_call(
        _post_attn_kernel,
        grid=(B, S // ts),
        in_specs=[tile(o.shape[2]), tile(D)] + [full(a) for a in consts_a]
        + [pl.BlockSpec((1,) + kv_mem.shape[1:], lambda b, i: (b, 0, 0))] + [full(a) for a in consts_b],
        out_specs=[tile(D), tile(D), pl.BlockSpec((1, E, ts), lambda b, i: (b, 0, i))],
        out_shape=[jax.ShapeDtypeStruct((B, S, D), F32), jax.ShapeDtypeStruct((B, S, D), BF16),
                   jax.ShapeDtypeStruct((B, E, S), F32)],
        compiler_params=_cparams(2),
        name="post_attn",
    )(o, x, *consts_a, kv_mem, *consts_b)


def _select_kernel(aff_ref, slot_ref, *, cap, chunks):
    a = aff_ref[0]
    rows = a.shape[0]
    bits = pltpu.bitcast(a, jnp.int32)

    r_i = lax.broadcasted_iota(jnp.int32, (rows, rows), 0)
    c_i = lax.broadcasted_iota(jnp.int32, (rows, rows), 1)
    same = (r_i // chunks) == (c_i // chunks)
    bd_all = jnp.where(same, 1.0, 0.0).astype(BF16)
    bd_before = jnp.where(same & (c_i < r_i), 1.0, 0.0).astype(BF16)
    l_r = lax.broadcasted_iota(jnp.int32, (LANES, LANES), 0)
    l_c = lax.broadcasted_iota(jnp.int32, (LANES, LANES), 1)
    ones = jnp.ones((LANES, LANES), BF16)
    before = jnp.where(l_r < l_c, 1.0, 0.0).astype(BF16)

    def as01(mask):
        return jnp.where(mask, 1.0, 0.0).astype(BF16)

    def expert_count(x01):
        return _dot(bd_all, _dot(x01, ones).astype(BF16))

    def prefix_excl(x01):
        return _dot(x01, before) + _dot(bd_before, _dot(x01, ones).astype(BF16))

    def step(i, theta):
        cand = theta | (jnp.int32(1) << (30 - i))
        cnt = expert_count(as01(bits >= cand))
        return jnp.where(cnt >= cap, cand, theta)

    theta = lax.fori_loop(0, 31, step, jnp.zeros(bits.shape, jnp.int32))
    gt = bits > theta
    eq = bits == theta
    need = cap - expert_count(as01(gt))
    sel = gt | (eq & (prefix_excl(as01(eq)) < need))
    pos = prefix_excl(as01(sel))
    slot_ref[0] = jnp.where(sel, pos, -1.0).astype(jnp.int32)


def _select(aff2, cap, chunks):
    B, rows, _ = aff2.shape
    return pl.pallas_call(
        functools.partial(_select_kernel, cap=cap, chunks=chunks),
        grid=(B,),
        in_specs=[pl.BlockSpec((1, rows, LANES), lambda b: (b, 0, 0))],
        out_specs=pl.BlockSpec((1, rows, LANES), lambda b: (b, 0, 0)),
        out_shape=jax.ShapeDtypeStruct((B, rows, LANES), jnp.int32),
        compiler_params=_cparams(1),
        name="select",
    )(aff2)


def _expert_kernel(slot_ref, aff_ref, h_ref, wg_ref, wu_ref, wd_ref, y_ref, *, cap):
    slot = slot_ref[0, 0]
    S = slot.shape[1]
    hit = lax.broadcasted_iota(jnp.int32, (cap, S), 0) == slot
    gate = jnp.sum(jnp.where(hit, aff_ref[0, 0], 0.0), axis=-1, keepdims=True)
    onehot = jnp.where(hit, 1.0, 0.0).astype(BF16)
    x_in = _dot(onehot, h_ref[0]).astype(BF16)
    a = _dot(x_in, wg_ref[0])
    u = _dot(x_in, wu_ref[0])
    hm = (a * (1.0 / (1.0 + jnp.exp(-a))) * u).astype(BF16)
    y_ref[0, 0] = (_dot(hm, wd_ref[0]) * gate).astype(BF16)


def _experts(slot_row, aff_row, h3, wg, wu, wd, cap):
    B, S, D = h3.shape
    E = N_EXPERTS
    F = wg.shape[2]
    row = pl.BlockSpec((1, 1, 1, S), lambda b, e: (b, e, 0, 0))
    return pl.pallas_call(
        functools.partial(_expert_kernel, cap=cap),
        grid=(B, E),
        in_specs=[row, row, pl.BlockSpec((1, S, D), lambda b, e: (b, 0, 0)),
                  pl.BlockSpec((1, D, F), lambda b, e: (e, 0, 0)), pl.BlockSpec((1, D, F), lambda b, e: (e, 0, 0)),
                  pl.BlockSpec((1, F, D), lambda b, e: (e, 0, 0))],
        out_specs=pl.BlockSpec((1, 1, cap, D), lambda b, e: (b, e, 0, 0)),
        out_shape=jax.ShapeDtypeStruct((B, E, cap, D), BF16),
        compiler_params=_cparams(2),
        name="experts",
    )(slot_row, aff_row, h3, wg, wu, wd)


def _combine_kernel(x_ref, slot_ref, y_ref, lnf_ref, o_ref, *, cap, final):
    tc = x_ref.shape[1]
    acc = x_ref[0]
    slot_t = slot_ref[0]
    col = lax.broadcasted_iota(jnp.int32, (tc, cap), 1)
    for e in range(N_EXPERTS):
        onehot = jnp.where(slot_t[:, e:e + 1] == col, 1.0, 0.0).astype(BF16)
        acc = acc + _dot(onehot, y_ref[0, e])
    if final:
        acc = _rms(acc) * lnf_ref[...]
    o_ref[0] = acc


def _combine(x2, slot_t, y, ln_final, cap, tc, final):
    B, S, D = x2.shape
    E = N_EXPERTS
    return pl.pallas_call(
        functools.partial(_combine_kernel, cap=cap, final=final),
        grid=(B, S // tc),
        in_specs=[pl.BlockSpec((1, tc, D), lambda b, i: (b, i, 0)), pl.BlockSpec((1, tc, E), lambda b, i: (b, i, 0)),
                  pl.BlockSpec((1, E, cap, D), lambda b, i: (b, 0, 0, 0)),
                  pl.BlockSpec(ln_final.shape, lambda b, i: (0, 0))],
        out_specs=pl.BlockSpec((1, tc, D), lambda b, i: (b, i, 0)),
        out_shape=jax.ShapeDtypeStruct((B, S, D), F32),
        compiler_params=_cparams(2),
        name="combine",
    )(x2, slot_t, y, ln_final)


def _layer_weights(l, p):
    d_model = p["w_in"].shape[1]
    w_in = p["w_in"][l]
    o1 = GQA_HEADS * GQA_HEAD_DIM
    o2 = o1 + GQA_KV_HEADS * GQA_HEAD_DIM
    o3 = o2 + GQA_KV_HEADS * GQA_HEAD_DIM
    o4 = o3 + MLA_Q_RANK
    o5 = o4 + MLA_KV_RANK
    heads = lambda w, n: w.reshape(w.shape[0], n, w.shape[1] // n)
    flat = lambda w: w.reshape(w.shape[0], -1)
    wq = flat(_lay_gqa(heads(w_in[:, :o1], GQA_HEADS)))
    wk = flat(_lay_gqa(heads(w_in[:, o1:o2], GQA_KV_HEADS)))
    wv = heads(w_in[:, o2:o3], GQA_KV_HEADS)
    wv = flat(_lay_v(wv))
    w_kr = w_in[:, o5:]
    wkr = _lay_mla(jnp.zeros((d_model, MLA_NOPE_DIM), F32), w_kr)
    wcat = jnp.concatenate([wq, wk, wv, w_in[:, o3:o4], w_in[:, o4:o5], wkr], axis=1).astype(BF16)

    wqb = heads(p["w_q_b"][l], MLA_HEADS)
    wqb = flat(_lay_mla(wqb[..., :MLA_NOPE_DIM], wqb[..., MLA_NOPE_DIM:])).astype(BF16)
    wkvb = heads(p["w_kv_b"][l], MLA_HEADS)
    k_nope = wkvb[..., :MLA_NOPE_DIM]
    wkb = flat(_lay_mla(k_nope, jnp.zeros(k_nope.shape[:-1] + (MLA_ROPE_DIM,), F32))).astype(BF16)
    wvb = flat(_lay_v(wkvb[..., MLA_NOPE_DIM:])).astype(BF16)

    q_scale = GQA_HEAD_DIM ** -0.5 * LOG2E
    gq = _lay_gqa(p["gqa_q_norm"][l] * q_scale).reshape(1, LANES)
    gk = _lay_gqa(p["gqa_k_norm"][l]).reshape(1, LANES)
    row = lambda v: v.reshape(1, -1)
    return {
        "ln_mix": row(p["ln_mix"][l]), "wcat": wcat, "wqb": wqb, "wkb": wkb, "wvb": wvb, "gq": gq, "gk": gk,
        "gql": row(p["mla_q_norm"][l]), "gkv": row(p["mla_kv_norm"][l]),
        "on": jnp.concatenate([p["out_norm_gqa"][l], p["out_norm_mla"][l]]).reshape(1, -1),
        "w_o": p["w_o"][l].astype(BF16), "ln_mem": row(p["ln_mem"][l]), "ln_mem_kv": row(p["ln_mem_kv"][l]),
        "w_mem_q": p["w_mem_q"][l].astype(BF16), "w_mem_kv": p["w_mem_kv"][l].astype(BF16),
        "w_mem_o": p["w_mem_o"][l].astype(BF16), "ln_ffn": row(p["ln_ffn"][l]),
        "w_router_t": p["w_router"][l].T,
        "w_gate": p["w_gate"][l].astype(BF16), "w_up": p["w_up"][l].astype(BF16),
        "w_down": p["w_down"][l].astype(BF16),
    }


def _tables(seq_len):
    cos_g, sin_g, cos_m, sin_m = _rope_tables(seq_len)
    mq_scale = MLA_QK_DIM ** -0.5 * LOG2E
    one_g = jnp.zeros((GQA_KV_HEADS, LANES), F32).at[:, ONE_LANE].set(1.0)
    one_m = jnp.zeros((MLA_HEADS, LANES), F32).at[:, ONE_LANE].set(1.0)
    return {"cos_g": cos_g, "sin_g": sin_g, "cos_m": cos_m, "sin_m": sin_m,
            "cos_mq": cos_m * mq_scale, "sin_mq": sin_m * mq_scale,
            "one_g": one_g.reshape(1, -1), "one_m": one_m.reshape(1, -1)}


def _pick(n, pref):
    t = min(n, pref)
    assert n % t == 0, (n, t)
    return t


def kernel(x, mem, ln_mix, w_in, gqa_q_norm, gqa_k_norm, mla_q_norm, mla_kv_norm, w_q_b, w_kv_b, out_norm_gqa,
           out_norm_mla, w_o, ln_mem, ln_mem_kv, w_mem_q, w_mem_kv, w_mem_o, ln_ffn, w_router, w_gate, w_up,
           w_down, ln_final):
    p = dict(ln_mix=ln_mix, w_in=w_in, gqa_q_norm=gqa_q_norm, gqa_k_norm=gqa_k_norm, mla_q_norm=mla_q_norm,
             mla_kv_norm=mla_kv_norm, w_q_b=w_q_b, w_kv_b=w_kv_b, out_norm_gqa=out_norm_gqa,
             out_norm_mla=out_norm_mla, w_o=w_o, ln_mem=ln_mem, ln_mem_kv=ln_mem_kv, w_mem_q=w_mem_q,
             w_mem_kv=w_mem_kv, w_mem_o=w_mem_o, ln_ffn=ln_ffn, w_router=w_router, w_gate=w_gate, w_up=w_up,
             w_down=w_down)
    B, S, D = x.shape
    depth = w_in.shape[0]
    E = N_EXPERTS
    assert S % LANES == 0 and S % GRID_W == 0
    cap = EC_CAPACITY_FACTOR * S // E
    chunks = S // LANES
    tabs = _tables(S)
    ln_final2 = ln_final.reshape(1, -1)
    ts_in, ts_post, tq, tk, tc = _pick(S, 256), _pick(S, 512), _pick(S, 1024), _pick(S, 512), _pick(S, 256)

    for l in range(depth):
        lw = _layer_weights(l, p)
        q_all, k_all, v_all = _mixer_in(x, lw, tabs, ts_in)
        o = _attention(q_all, k_all, v_all, tq, tk)
        kv_mem = _mem_kv(mem, lw["ln_mem_kv"], lw["w_mem_kv"])
        x2, h3, aff = _post_attn(o, x, kv_mem, lw, ts_post)
        slot = _select(aff.reshape(B, E * chunks, LANES), cap, chunks).reshape(B, E, S)
        y = _experts(slot.reshape(B, E, 1, S), aff.reshape(B, E, 1, S), h3, lw["w_gate"], lw["w_up"], lw["w_down"], cap)
        x = _combine(x2, jnp.swapaxes(slot, 1, 2), y, ln_final2, cap, tc, final=(l == depth - 1))
    return x
```

```python
import functools
import math

import numpy as np
import jax
import jax.numpy as jnp
from jax import lax
from jax.experimental import pallas as pl
from jax.experimental.pallas import tpu as pltpu

F32 = jnp.float32
BF16 = jnp.bfloat16

GRID_W = 64
ROPE_THETA = 10000.0
EPS = 1e-6
GQA_HEADS = 8
GQA_KV_HEADS = 2
GQA_GROUP = GQA_HEADS // GQA_KV_HEADS
GQA_HEAD_DIM = 64
MLA_HEADS = 8
MLA_Q_RANK = 256
MLA_KV_RANK = 128
MLA_NOPE_DIM = 64
MLA_ROPE_DIM = 32
MLA_V_DIM = 64
MLA_QK_DIM = MLA_NOPE_DIM + MLA_ROPE_DIM
MEM_HEADS = 4
MEM_HEAD_DIM = 128
N_EXPERTS = 16
EC_CAPACITY_FACTOR = 2

LANES = 128
LOG2E = math.log2(math.e)
VMEM_LIMIT = 56 * 1024 * 1024

N_HEADS = GQA_HEADS + MLA_HEADS
N_PAIRS = N_HEADS // 2
GQA_PAIRS = GQA_HEADS // 2


def _cparams(n_axes):
    return pltpu.CompilerParams(dimension_semantics=("arbitrary",) * n_axes, vmem_limit_bytes=VMEM_LIMIT)


def _rms(x, eps=EPS):
    return x * lax.rsqrt(jnp.mean(x * x, axis=-1, keepdims=True) + eps)


def _dot(a, b):
    return jnp.dot(a, b, preferred_element_type=F32)


def _dot_nt(a, b):
    return lax.dot_general(a, b, (((1,), (1,)), ((), ())), preferred_element_type=F32)


V_DIM = 64
ONE_LANE = V_DIM


def _zeros_like_cols(w, n):
    return jnp.zeros(w.shape[:-1] + (n,), w.dtype)


def _lay_gqa(w):
    z = _zeros_like_cols(w, 32)
    return jnp.concatenate([w[..., :32], z, w[..., 32:], z], axis=-1)


def _lay_mla(nope, rope):
    z = _zeros_like_cols(nope, 16)
    return jnp.concatenate([nope[..., :32], rope[..., :16], z, nope[..., 32:], rope[..., 16:], z], axis=-1)


def _lay_v(v):
    return jnp.concatenate([v, _zeros_like_cols(v, LANES - V_DIM)], axis=-1)


def _rope_tables(seq_len):
    rows = seq_len // GRID_W
    row = jnp.repeat(jnp.arange(rows, dtype=F32), GRID_W)
    col = jnp.tile(jnp.arange(GRID_W, dtype=F32), rows)

    def angles(rot_dim):
        axis_dim = rot_dim // 2
        inv_freq = ROPE_THETA ** (-jnp.arange(0, axis_dim, 2, dtype=F32) / axis_dim)
        ang = jnp.concatenate([row[:, None] * inv_freq[None, :], col[:, None] * inv_freq[None, :]], axis=-1)
        return jnp.cos(ang), jnp.sin(ang)

    cg, sg = angles(GQA_HEAD_DIM)
    cm, sm = angles(MLA_ROPE_DIM)
    cos_g = _lay_gqa(jnp.concatenate([cg, cg], axis=-1))
    sin_g = _lay_gqa(jnp.concatenate([-sg, sg], axis=-1))
    one = jnp.ones((seq_len, MLA_NOPE_DIM), F32)
    cos_m = _lay_mla(one, jnp.concatenate([cm, cm], axis=-1))
    sin_m = _lay_mla(0.0 * one, jnp.concatenate([-sm, sm], axis=-1))
    return cos_g, sin_g, cos_m, sin_m


_W_SPLITS = (GQA_HEADS * LANES, GQA_KV_HEADS * LANES, GQA_KV_HEADS * LANES, MLA_Q_RANK, MLA_KV_RANK, LANES)
_W_OFFS = tuple(int(v) for v in np.cumsum((0,) + _W_SPLITS))


def _mixer_in_kernel(x_ref, ln_ref, wcat_ref, wqb_ref, wkb_ref, wvb_ref, gq_ref, gk_ref, gql_ref, gkv_ref,
                     cg_ref, sg_ref, cmq_ref, smq_ref, cmk_ref, smk_ref, oneg_ref, onem_ref,
                     q_ref, k_ref, v_ref):
    x = x_ref[0]
    h = (_rms(x) * ln_ref[...]).astype(BF16)
    proj = _dot(h, wcat_ref[...])
    o = _W_OFFS
    cg, sg = cg_ref[...], sg_ref[...]

    def head_norm_rope(blk, gain):
        ss = jnp.sum(blk * blk, axis=-1, keepdims=True) * (1.0 / GQA_HEAD_DIM)
        y = blk * lax.rsqrt(ss + EPS) * gain
        return y * cg + pltpu.roll(y, 64, 1) * sg

    for j in range(GQA_HEADS):
        blk = proj[:, o[0] + j * LANES:o[0] + (j + 1) * LANES]
        q_ref[0, :, j * LANES:(j + 1) * LANES] = head_norm_rope(blk, gq_ref[...]).astype(BF16)
    for j in range(GQA_KV_HEADS):
        blk = proj[:, o[1] + j * LANES:o[1] + (j + 1) * LANES]
        k_ref[0, :, j * LANES:(j + 1) * LANES] = head_norm_rope(blk, gk_ref[...]).astype(BF16)
    n_vg = GQA_KV_HEADS * LANES
    v_ref[0, :, 0:n_vg] = (proj[:, o[2]:o[3]] + oneg_ref[...]).astype(BF16)

    c_q = (_rms(proj[:, o[3]:o[4]]) * gql_ref[...]).astype(BF16)
    qm = _dot(c_q, wqb_ref[...])
    cmq, smq = cmq_ref[...], smq_ref[...]
    for j in range(MLA_HEADS):
        blk = qm[:, j * LANES:(j + 1) * LANES]
        q_ref[0, :, (GQA_HEADS + j) * LANES:(GQA_HEADS + j + 1) * LANES] = (
            blk * cmq + pltpu.roll(blk, 64, 1) * smq).astype(BF16)

    c_kv = (_rms(proj[:, o[4]:o[5]]) * gkv_ref[...]).astype(BF16)
    kn = _dot(c_kv, wkb_ref[...])
    vm = _dot(c_kv, wvb_ref[...]) + onem_ref[...]
    kr = proj[:, o[5]:o[6]]
    kr = kr * cmk_ref[...] + pltpu.roll(kr, 64, 1) * smk_ref[...]
    for j in range(MLA_HEADS):
        k_ref[0, :, (GQA_KV_HEADS + j) * LANES:(GQA_KV_HEADS + j + 1) * LANES] = (
            kn[:, j * LANES:(j + 1) * LANES] + kr).astype(BF16)
    v_ref[0, :, n_vg:] = vm.astype(BF16)


def _mixer_in(x, lw, tabs, ts):
    B, S, D = x.shape
    nq, nk = N_HEADS * LANES, (GQA_KV_HEADS + MLA_HEADS) * LANES
    nv = nk
    full = lambda a: pl.BlockSpec(a.shape, lambda b, i: (0,) * a.ndim)
    tab = pl.BlockSpec((ts, LANES), lambda b, i: (i, 0))
    consts = (lw["ln_mix"], lw["wcat"], lw["wqb"], lw["wkb"], lw["wvb"], lw["gq"], lw["gk"], lw["gql"], lw["gkv"])
    return pl.pallas_call(
        _mixer_in_kernel,
        grid=(B, S // ts),
        in_specs=[pl.BlockSpec((1, ts, D), lambda b, i: (b, i, 0))] + [full(a) for a in consts]
        + [tab] * 6 + [full(tabs["one_g"]), full(tabs["one_m"])],
        out_specs=[pl.BlockSpec((1, ts, nq), lambda b, i: (b, i, 0)),
                   pl.BlockSpec((1, ts, nk), lambda b, i: (b, i, 0)),
                   pl.BlockSpec((1, ts, nv), lambda b, i: (b, i, 0))],
        out_shape=[jax.ShapeDtypeStruct((B, S, nq), BF16), jax.ShapeDtypeStruct((B, S, nk), BF16),
                   jax.ShapeDtypeStruct((B, S, nv), BF16)],
        compiler_params=_cparams(2),
        name="mixer_in",
    )(x, *consts, tabs["cos_g"], tabs["sin_g"], tabs["cos_mq"], tabs["sin_mq"], tabs["cos_m"], tabs["sin_m"],
      tabs["one_g"], tabs["one_m"])


SCORE_BOUND_MAX = 40.0
BOUND_SLACK = 1.02


def _dot_tn(a, b):
    return lax.dot_general(a, b, (((0,), (0,)), ((), ())), preferred_element_type=F32)


def _attn_kernel(q_ref, ka_ref, kb_ref, va_ref, vb_ref, o_ref, kmax_ref, *, tk):
    tq = q_ref.shape[1]
    n_chunks = ka_ref.shape[1] // tk
    ones8 = jnp.ones((8, LANES), BF16)
    k_refs = (ka_ref, kb_ref)
    v_refs = (va_ref, vb_ref)

    @pl.when(pl.program_id(2) == 0)
    def _():
        for h in range(2):
            kk = k_refs[h][0]
            ksq = _dot_nt(ones8, kk * kk)
            kmax_ref[h] = jnp.broadcast_to(jnp.max(ksq, axis=-1, keepdims=True), (8, LANES))

    qs = (q_ref[0, :, 0:LANES], q_ref[0, :, LANES:2 * LANES])
    bounds = []
    for h in range(2):
        qsq = _dot_nt(ones8, qs[h] * qs[h])[0:1]
        bounds.append(jnp.sqrt(qsq * kmax_ref[h][0:1, 0:1]) * BOUND_SLACK)
    bound_max = jnp.max(jnp.maximum(bounds[0], bounds[1]))

    def finish(accs):
        outs = [a[0:V_DIM] * (1.0 / a[ONE_LANE:ONE_LANE + 1]) for a in accs]
        o_ref[0] = jnp.concatenate(outs, axis=0).T

    def bounded():
        accs = [jnp.zeros((LANES, tq), F32), jnp.zeros((LANES, tq), F32)]
        for c in range(n_chunks):
            for h in range(2):
                ks = k_refs[h][0, c * tk:(c + 1) * tk, :]
                vs = v_refs[h][0, c * tk:(c + 1) * tk, :]
                pt = jnp.exp2(_dot_nt(ks, qs[h]) - bounds[h]).astype(BF16)
                accs[h] = accs[h] + _dot_tn(vs, pt)
        finish(accs)

    def running_max():
        accs = []
        for h in range(2):
            def body(c, carry):
                m, acc = carry
                start = pl.multiple_of(c * tk, tk)
                ks = k_refs[h][0, pl.ds(start, tk), :]
                vs = v_refs[h][0, pl.ds(start, tk), :]
                st = _dot_nt(ks, qs[h])
                m_new = jnp.maximum(m, jnp.max(st, axis=0, keepdims=True))
                pt = jnp.exp2(st - m_new).astype(BF16)
                return m_new, jnp.exp2(m - m_new) * acc + _dot_tn(vs, pt)

            init = (jnp.full((1, tq), -jnp.inf, F32), jnp.zeros((LANES, tq), F32))
            accs.append(lax.fori_loop(0, n_chunks, body, init)[1])
        finish(accs)

    pl.when(bound_max <= SCORE_BOUND_MAX)(bounded)
    pl.when(jnp.logical_not(bound_max <= SCORE_BOUND_MAX))(running_max)


def _attention(q_all, k_all, v_all, tq, tk):
    B, S, _ = q_all.shape
    pairs_per_kv = GQA_GROUP // 2

    def kv_col(p, second):
        return jnp.where(p < GQA_PAIRS, p // pairs_per_kv, GQA_KV_HEADS + 2 * (p - GQA_PAIRS) + second)

    kv_spec = lambda second: pl.BlockSpec((1, S, LANES), lambda b, p, i: (b, 0, kv_col(p, second)))
    return pl.pallas_call(
        functools.partial(_attn_kernel, tk=tk),
        grid=(B, N_PAIRS, S // tq),
        in_specs=[pl.BlockSpec((1, tq, 2 * LANES), lambda b, p, i: (b, i, p)),
                  kv_spec(0), kv_spec(1), kv_spec(0), kv_spec(1)],
        out_specs=pl.BlockSpec((1, tq, LANES), lambda b, p, i: (b, i, p)),
        out_shape=jax.ShapeDtypeStruct((B, S, N_PAIRS * LANES), F32),
        scratch_shapes=[pltpu.VMEM((2, 8, LANES), F32)],
        compiler_params=_cparams(3),
        name="attention",
    )(q_all, k_all, k_all, v_all, v_all)


def _mem_kv_kernel(m_ref, ln_ref, w_ref, o_ref):
    h = (_rms(m_ref[0]) * ln_ref[...]).astype(BF16)
    o_ref[0] = _dot(h, w_ref[...]).astype(BF16)


def _mem_kv(mem, ln, w):
    B, M, D = mem.shape
    n = w.shape[1]
    return pl.pallas_call(
        _mem_kv_kernel,
        grid=(B,),
        in_specs=[pl.BlockSpec((1, M, D), lambda b: (b, 0, 0)), pl.BlockSpec(ln.shape, lambda b: (0, 0)),
                  pl.BlockSpec(w.shape, lambda b: (0, 0))],
        out_specs=pl.BlockSpec((1, M, n), lambda b: (b, 0, 0)),
        out_shape=jax.ShapeDtypeStruct((B, M, n), BF16),
        compiler_params=_cparams(1),
        name="mem_kv",
    )(mem, ln, w)


def _post_attn_kernel(o_ref, x_ref, on_ref, wo_ref, lnm_ref, wmq_ref, kv_ref, wmo_ref, lnf_ref, wr_ref,
                      x2_ref, h3_ref, aff_ref):
    o = o_ref[0]
    half = o.shape[1] // 2
    merged = (jnp.concatenate([_rms(o[:, :half]), _rms(o[:, half:])], axis=-1) * on_ref[...]).astype(BF16)
    x1 = x_ref[0] + _dot(merged, wo_ref[...])

    h2 = (_rms(x1) * lnm_ref[...]).astype(BF16)
    q = (_dot(h2, wmq_ref[...]) * (MEM_HEAD_DIM ** -0.5 * LOG2E)).astype(BF16)
    kv = kv_ref[0]
    n_mem = MEM_HEADS * MEM_HEAD_DIM
    outs = []
    for hh in range(MEM_HEADS):
        lo, hi = hh * MEM_HEAD_DIM, (hh + 1) * MEM_HEAD_DIM
        s = _dot_nt(q[:, lo:hi], kv[:, lo:hi])
        p = jnp.exp2(s - jnp.max(s, axis=-1, keepdims=True))
        l = jnp.sum(p, axis=-1, keepdims=True)
        outs.append(_dot(p.astype(BF16), kv[:, n_mem + lo:n_mem + hi]) * (1.0 / l))
    oc = jnp.concatenate(outs, axis=-1).astype(BF16)
    x2 = x1 + _dot(oc, wmo_ref[...])
    x2_ref[0] = x2

    h3 = _rms(x2) * lnf_ref[...]
    h3_ref[0] = h3.astype(BF16)
    logits = lax.dot_general(wr_ref[...], h3, (((1,), (1,)), ((), ())), preferred_element_type=F32,
                             precision=lax.Precision.HIGHEST)
    e = jnp.exp(logits - jnp.max(logits, axis=0, keepdims=True))
    aff_ref[0] = e * (1.0 / jnp.sum(e, axis=0, keepdims=True))


def _post_attn(o, x, kv_mem, lw, ts):
    B, S, D = x.shape
    E = N_EXPERTS
    consts_a = (lw["on"], lw["w_o"], lw["ln_mem"], lw["w_mem_q"])
    consts_b = (lw["w_mem_o"], lw["ln_ffn"], lw["w_router_t"])
    full = lambda a: pl.BlockSpec(a.shape, lambda b, i: (0,) * a.ndim)
    tile = lambda n: pl.BlockSpec((1, ts, n), lambda b, i: (b, i, 0))
    return pl.pallas_call(
        _post_attn_kernel,
        grid=(B, S // ts),
        in_specs=[tile(o.shape[2]), tile(D)] + [full(a) for a in consts_a]
        + [pl.BlockSpec((1,) + kv_mem.shape[1:], lambda b, i: (b, 0, 0))] + [full(a) for a in consts_b],
        out_specs=[tile(D), tile(D), pl.BlockSpec((1, E, ts), lambda b, i: (b, 0, i))],
        out_shape=[jax.ShapeDtypeStruct((B, S, D), F32), jax.ShapeDtypeStruct((B, S, D), BF16),
                   jax.ShapeDtypeStruct((B, E, S), F32)],
        compiler_params=_cparams(2),
        name="post_attn",
    )(o, x, *consts_a, kv_mem, *consts_b)


def _select_kernel(aff_ref, slot_ref, pos_ref, *, cap, chunks):
    a = aff_ref[0]
    rows = a.shape[0]
    bits = pltpu.bitcast(a, jnp.int32)

    r_i = lax.broadcasted_iota(jnp.int32, (rows, rows), 0)
    c_i = lax.broadcasted_iota(jnp.int32, (rows, rows), 1)
    same = (r_i // chunks) == (c_i // chunks)
    bd_all = jnp.where(same, 1.0, 0.0).astype(BF16)
    bd_before = jnp.where(same & (c_i < r_i), 1.0, 0.0).astype(BF16)
    l_r = lax.broadcasted_iota(jnp.int32, (LANES, LANES), 0)
    l_c = lax.broadcasted_iota(jnp.int32, (LANES, LANES), 1)
    ones = jnp.ones((LANES, LANES), BF16)
    before = jnp.where(l_r < l_c, 1.0, 0.0).astype(BF16)

    def as01(mask):
        return jnp.where(mask, 1.0, 0.0).astype(BF16)

    def expert_count(x01):
        return _dot(bd_all, _dot(x01, ones).astype(BF16))

    def prefix_excl(x01):
        return _dot(x01, before) + _dot(bd_before, _dot(x01, ones).astype(BF16))

    def step(i, theta):
        cand = theta | (jnp.int32(1) << (30 - i))
        cnt = expert_count(as01(bits >= cand))
        return jnp.where(cnt >= cap, cand, theta)

    theta = lax.fori_loop(0, 31, step, jnp.zeros(bits.shape, jnp.int32))
    gt = bits > theta
    eq = bits == theta
    need = cap - expert_count(as01(gt))
    sel = gt | (eq & (prefix_excl(as01(eq)) < need))
    pos = prefix_excl(as01(sel))
    slot_ref[0] = jnp.where(sel, pos, -1.0).astype(jnp.int32)
    pos_ref[0] = pos.astype(jnp.int32)


def _select(aff2, cap, chunks):
    B, rows, _ = aff2.shape
    blk = pl.BlockSpec((1, rows, LANES), lambda b: (b, 0, 0))
    return pl.pallas_call(
        functools.partial(_select_kernel, cap=cap, chunks=chunks),
        grid=(B,),
        in_specs=[blk],
        out_specs=[blk, blk],
        out_shape=[jax.ShapeDtypeStruct((B, rows, LANES), jnp.int32)] * 2,
        compiler_params=_cparams(1),
        name="select",
    )(aff2)


GATHER_TILE = 2 * LANES
GATHER_WIN = GATHER_TILE + 8
SCATTER_TILE = LANES
SCATTER_WIN = 2 * LANES


def _y_rows(cap):
    return -(-cap // LANES) * LANES + LANES


def _expert_kernel(ps_ref, slot_ref, aff_ref, h_ref, wg_ref, wu_ref, wd_ref, y_ref, xin_ref, gate_ref, *, cap, chunks):
    base = (pl.program_id(0) * N_EXPERTS + pl.program_id(1)) * chunks
    S = h_ref.shape[1]
    xin_ref[...] = jnp.zeros_like(xin_ref)
    gate_ref[...] = jnp.zeros_like(gate_ref)
    row = lax.broadcasted_iota(jnp.int32, (GATHER_WIN, GATHER_TILE), 0)
    for t in range(S // GATHER_TILE):
        lo, hi = t * GATHER_TILE, (t + 1) * GATHER_TILE
        start = pl.multiple_of((ps_ref[base + t * (GATHER_TILE // LANES)] >> 3) << 3, 8)
        hit = (row + start) == slot_ref[0, 0, :, lo:hi]
        onehot = jnp.where(hit, 1.0, 0.0).astype(BF16)
        xin_ref[pl.ds(start, GATHER_WIN), :] += _dot(onehot, h_ref[0, lo:hi, :])
        g = jnp.sum(jnp.where(hit, aff_ref[0, 0, :, lo:hi], 0.0), axis=-1, keepdims=True)
        gate_ref[pl.ds(start, GATHER_WIN), :] += jnp.broadcast_to(g, (GATHER_WIN, LANES))
    x_in = xin_ref[0:cap, :].astype(BF16)
    a = _dot(x_in, wg_ref[0])
    u = _dot(x_in, wu_ref[0])
    hm = (a * (1.0 / (1.0 + jnp.exp(-a))) * u).astype(BF16)
    y_ref[0, 0, 0:cap, :] = (_dot(hm, wd_ref[0]) * gate_ref[0:cap, 0:1]).astype(BF16)
    y_ref[0, 0, cap:, :] = jnp.zeros((y_ref.shape[2] - cap, y_ref.shape[3]), BF16)


def _experts(pstart, slot_row, aff_row, h3, wg, wu, wd, cap, chunks):
    B, S, D = h3.shape
    E = N_EXPERTS
    F = wg.shape[2]
    assert S % GATHER_TILE == 0 and cap % 8 == 0
    row = pl.BlockSpec((1, 1, 1, S), lambda b, e, ps: (b, e, 0, 0))
    return pl.pallas_call(
        functools.partial(_expert_kernel, cap=cap, chunks=chunks),
        grid_spec=pltpu.PrefetchScalarGridSpec(
            num_scalar_prefetch=1,
            grid=(B, E),
            in_specs=[row, row, pl.BlockSpec((1, S, D), lambda b, e, ps: (b, 0, 0)),
                      pl.BlockSpec((1, D, F), lambda b, e, ps: (e, 0, 0)),
                      pl.BlockSpec((1, D, F), lambda b, e, ps: (e, 0, 0)),
                      pl.BlockSpec((1, F, D), lambda b, e, ps: (e, 0, 0))],
            out_specs=pl.BlockSpec((1, 1, _y_rows(cap), D), lambda b, e, ps: (b, e, 0, 0)),
            scratch_shapes=[pltpu.VMEM((cap + GATHER_WIN, D), F32), pltpu.VMEM((cap + GATHER_WIN, LANES), F32)]),
        out_shape=jax.ShapeDtypeStruct((B, E, _y_rows(cap), D), BF16),
        compiler_params=_cparams(2),
        name="experts",
    )(pstart, slot_row, aff_row, h3, wg, wu, wd)


def _combine_kernel(ps_ref, x_ref, slot_ref, y_ref, lnf_ref, o_ref, *, chunks, final):
    tc = x_ref.shape[1]
    b, i = pl.program_id(0), pl.program_id(1)
    col = lax.broadcasted_iota(jnp.int32, (SCATTER_TILE, SCATTER_WIN), 1)
    for sub in range(tc // SCATTER_TILE):
        lo, hi = sub * SCATTER_TILE, (sub + 1) * SCATTER_TILE
        chunk = i * (tc // SCATTER_TILE) + sub
        slot_t = slot_ref[0, lo:hi, :]
        hots, wins = [], []
        for e in range(N_EXPERTS):
            first = (ps_ref[(b * N_EXPERTS + e) * chunks + chunk] >> 7) << 7
            start = pl.multiple_of(jnp.minimum(first, y_ref.shape[2] - SCATTER_WIN), LANES)
            hots.append(jnp.where(slot_t[:, e:e + 1] == col + start, 1.0, 0.0).astype(BF16))
            wins.append(y_ref[0, e, pl.ds(start, SCATTER_WIN), :])
        acc = x_ref[0, lo:hi, :] + _dot(jnp.concatenate(hots, axis=1), jnp.concatenate(wins, axis=0))
        if final:
            acc = _rms(acc) * lnf_ref[...]
        o_ref[0, lo:hi, :] = acc


def _combine(pstart, x2, slot_t, y, ln_final, chunks, tc, final):
    B, S, D = x2.shape
    E = N_EXPERTS
    assert tc % SCATTER_TILE == 0 and SCATTER_TILE == LANES
    return pl.pallas_call(
        functools.partial(_combine_kernel, chunks=chunks, final=final),
        grid_spec=pltpu.PrefetchScalarGridSpec(
            num_scalar_prefetch=1,
            grid=(B, S // tc),
            in_specs=[pl.BlockSpec((1, tc, D), lambda b, i, ps: (b, i, 0)),
                      pl.BlockSpec((1, tc, E), lambda b, i, ps: (b, i, 0)),
                      pl.BlockSpec((1,) + y.shape[1:], lambda b, i, ps: (b, 0, 0, 0), pipeline_mode=pl.Buffered(1)),
                      pl.BlockSpec(ln_final.shape, lambda b, i, ps: (0, 0))],
            out_specs=pl.BlockSpec((1, tc, D), lambda b, i, ps: (b, i, 0))),
        out_shape=jax.ShapeDtypeStruct((B, S, D), F32),
        compiler_params=_cparams(2),
        name="combine",
    )(pstart, x2, slot_t, y, ln_final)


def _layer_weights(l, p):
    d_model = p["w_in"].shape[1]
    w_in = p["w_in"][l]
    o1 = GQA_HEADS * GQA_HEAD_DIM
    o2 = o1 + GQA_KV_HEADS * GQA_HEAD_DIM
    o3 = o2 + GQA_KV_HEADS * GQA_HEAD_DIM
    o4 = o3 + MLA_Q_RANK
    o5 = o4 + MLA_KV_RANK
    heads = lambda w, n: w.reshape(w.shape[0], n, w.shape[1] // n)
    flat = lambda w: w.reshape(w.shape[0], -1)
    wq = flat(_lay_gqa(heads(w_in[:, :o1], GQA_HEADS)))
    wk = flat(_lay_gqa(heads(w_in[:, o1:o2], GQA_KV_HEADS)))
    wv = heads(w_in[:, o2:o3], GQA_KV_HEADS)
    wv = flat(_lay_v(wv))
    w_kr = w_in[:, o5:]
    wkr = _lay_mla(jnp.zeros((d_model, MLA_NOPE_DIM), F32), w_kr)
    wcat = jnp.concatenate([wq, wk, wv, w_in[:, o3:o4], w_in[:, o4:o5], wkr], axis=1).astype(BF16)

    wqb = heads(p["w_q_b"][l], MLA_HEADS)
    wqb = flat(_lay_mla(wqb[..., :MLA_NOPE_DIM], wqb[..., MLA_NOPE_DIM:])).astype(BF16)
    wkvb = heads(p["w_kv_b"][l], MLA_HEADS)
    k_nope = wkvb[..., :MLA_NOPE_DIM]
    wkb = flat(_lay_mla(k_nope, jnp.zeros(k_nope.shape[:-1] + (MLA_ROPE_DIM,), F32))).astype(BF16)
    wvb = flat(_lay_v(wkvb[..., MLA_NOPE_DIM:])).astype(BF16)

    q_scale = GQA_HEAD_DIM ** -0.5 * LOG2E
    gq = _lay_gqa(p["gqa_q_norm"][l] * q_scale).reshape(1, LANES)
    gk = _lay_gqa(p["gqa_k_norm"][l]).reshape(1, LANES)
    row = lambda v: v.reshape(1, -1)
    return {
        "ln_mix": row(p["ln_mix"][l]), "wcat": wcat, "wqb": wqb, "wkb": wkb, "wvb": wvb, "gq": gq, "gk": gk,
        "gql": row(p["mla_q_norm"][l]), "gkv": row(p["mla_kv_norm"][l]),
        "on": jnp.concatenate([p["out_norm_gqa"][l], p["out_norm_mla"][l]]).reshape(1, -1),
        "w_o": p["w_o"][l].astype(BF16), "ln_mem": row(p["ln_mem"][l]), "ln_mem_kv": row(p["ln_mem_kv"][l]),
        "w_mem_q": p["w_mem_q"][l].astype(BF16), "w_mem_kv": p["w_mem_kv"][l].astype(BF16),
        "w_mem_o": p["w_mem_o"][l].astype(BF16), "ln_ffn": row(p["ln_ffn"][l]),
        "w_router_t": p["w_router"][l].T,
        "w_gate": p["w_gate"][l].astype(BF16), "w_up": p["w_up"][l].astype(BF16),
        "w_down": p["w_down"][l].astype(BF16),
    }


def _tables(seq_len):
    cos_g, sin_g, cos_m, sin_m = _rope_tables(seq_len)
    mq_scale = MLA_QK_DIM ** -0.5 * LOG2E
    one_g = jnp.zeros((GQA_KV_HEADS, LANES), F32).at[:, ONE_LANE].set(1.0)
    one_m = jnp.zeros((MLA_HEADS, LANES), F32).at[:, ONE_LANE].set(1.0)
    return {"cos_g": cos_g, "sin_g": sin_g, "cos_m": cos_m, "sin_m": sin_m,
            "cos_mq": cos_m * mq_scale, "sin_mq": sin_m * mq_scale,
            "one_g": one_g.reshape(1, -1), "one_m": one_m.reshape(1, -1)}


def _pick(n, pref):
    t = min(n, pref)
    assert n % t == 0, (n, t)
    return t


def kernel(x, mem, ln_mix, w_in, gqa_q_norm, gqa_k_norm, mla_q_norm, mla_kv_norm, w_q_b, w_kv_b, out_norm_gqa,
           out_norm_mla, w_o, ln_mem, ln_mem_kv, w_mem_q, w_mem_kv, w_mem_o, ln_ffn, w_router, w_gate, w_up,
           w_down, ln_final):
    p = dict(ln_mix=ln_mix, w_in=w_in, gqa_q_norm=gqa_q_norm, gqa_k_norm=gqa_k_norm, mla_q_norm=mla_q_norm,
             mla_kv_norm=mla_kv_norm, w_q_b=w_q_b, w_kv_b=w_kv_b, out_norm_gqa=out_norm_gqa,
             out_norm_mla=out_norm_mla, w_o=w_o, ln_mem=ln_mem, ln_mem_kv=ln_mem_kv, w_mem_q=w_mem_q,
             w_mem_kv=w_mem_kv, w_mem_o=w_mem_o, ln_ffn=ln_ffn, w_router=w_router, w_gate=w_gate, w_up=w_up,
             w_down=w_down)
    B, S, D = x.shape
    depth = w_in.shape[0]
    E = N_EXPERTS
    assert S % LANES == 0 and S % GRID_W == 0
    cap = EC_CAPACITY_FACTOR * S // E
    chunks = S // LANES
    tabs = _tables(S)
    ln_final2 = ln_final.reshape(1, -1)
    ts_in, ts_post, tq, tk, tc = _pick(S, 256), _pick(S, 512), _pick(S, 1024), _pick(S, 512), _pick(S, 256)

    for l in range(depth):
        lw = _layer_weights(l, p)
        q_all, k_all, v_all = _mixer_in(x, lw, tabs, ts_in)
        o = _attention(q_all, k_all, v_all, tq, tk)
        kv_mem = _mem_kv(mem, lw["ln_mem_kv"], lw["w_mem_kv"])
        x2, h3, aff = _post_attn(o, x, kv_mem, lw, ts_post)
        slot, pos = _select(aff.reshape(B, E * chunks, LANES), cap, chunks)
        slot = slot.reshape(B, E, S)
        pstart = pos[:, :, 0].reshape(B * E * chunks)
        y = _experts(pstart, slot.reshape(B, E, 1, S), aff.reshape(B, E, 1, S), h3, lw["w_gate"], lw["w_up"],
                     lw["w_down"], cap, chunks)
        x = _combine(pstart, x2, jnp.swapaxes(slot, 1, 2), y, ln_final2, chunks, tc, final=(l == depth - 1))
    return x
```

```python
import functools
import math

import numpy as np
import jax
import jax.numpy as jnp
from jax import lax
from jax.experimental import pallas as pl
from jax.experimental.pallas import tpu as pltpu

F32 = jnp.float32
BF16 = jnp.bfloat16

GRID_W = 64
ROPE_THETA = 10000.0
EPS = 1e-6
GQA_HEADS = 8
GQA_KV_HEADS = 2
GQA_GROUP = GQA_HEADS // GQA_KV_HEADS
GQA_HEAD_DIM = 64
MLA_HEADS = 8
MLA_Q_RANK = 256
MLA_KV_RANK = 128
MLA_NOPE_DIM = 64
MLA_ROPE_DIM = 32
MLA_V_DIM = 64
MLA_QK_DIM = MLA_NOPE_DIM + MLA_ROPE_DIM
MEM_HEADS = 4
MEM_HEAD_DIM = 128
N_EXPERTS = 16
EC_CAPACITY_FACTOR = 2

LANES = 128
LOG2E = math.log2(math.e)
VMEM_LIMIT = 56 * 1024 * 1024

N_HEADS = GQA_HEADS + MLA_HEADS
N_PAIRS = N_HEADS // 2
GQA_PAIRS = GQA_HEADS // 2


def _cparams(n_axes):
    return pltpu.CompilerParams(dimension_semantics=("arbitrary",) * n_axes, vmem_limit_bytes=VMEM_LIMIT)


def _rms(x, eps=EPS):
    return x * lax.rsqrt(jnp.mean(x * x, axis=-1, keepdims=True) + eps)


def _dot(a, b):
    return jnp.dot(a, b, preferred_element_type=F32)


def _dot_nt(a, b):
    return lax.dot_general(a, b, (((1,), (1,)), ((), ())), preferred_element_type=F32)


V_DIM = 64
ONE_LANE = V_DIM


def _zeros_like_cols(w, n):
    return jnp.zeros(w.shape[:-1] + (n,), w.dtype)


def _lay_gqa(w):
    z = _zeros_like_cols(w, 32)
    return jnp.concatenate([w[..., :32], z, w[..., 32:], z], axis=-1)


def _lay_mla(nope, rope):
    z = _zeros_like_cols(nope, 16)
    return jnp.concatenate([nope[..., :32], rope[..., :16], z, nope[..., 32:], rope[..., 16:], z], axis=-1)


def _lay_v(v):
    return jnp.concatenate([v, _zeros_like_cols(v, LANES - V_DIM)], axis=-1)


def _rope_tables(seq_len):
    rows = seq_len // GRID_W
    row = jnp.repeat(jnp.arange(rows, dtype=F32), GRID_W)
    col = jnp.tile(jnp.arange(GRID_W, dtype=F32), rows)

    def angles(rot_dim):
        axis_dim = rot_dim // 2
        inv_freq = ROPE_THETA ** (-jnp.arange(0, axis_dim, 2, dtype=F32) / axis_dim)
        ang = jnp.concatenate([row[:, None] * inv_freq[None, :], col[:, None] * inv_freq[None, :]], axis=-1)
        return jnp.cos(ang), jnp.sin(ang)

    cg, sg = angles(GQA_HEAD_DIM)
    cm, sm = angles(MLA_ROPE_DIM)
    cos_g = _lay_gqa(jnp.concatenate([cg, cg], axis=-1))
    sin_g = _lay_gqa(jnp.concatenate([-sg, sg], axis=-1))
    one = jnp.ones((seq_len, MLA_NOPE_DIM), F32)
    cos_m = _lay_mla(one, jnp.concatenate([cm, cm], axis=-1))
    sin_m = _lay_mla(0.0 * one, jnp.concatenate([-sm, sm], axis=-1))
    return cos_g, sin_g, cos_m, sin_m


_W_SPLITS = (GQA_HEADS * LANES, GQA_KV_HEADS * LANES, GQA_KV_HEADS * LANES, MLA_Q_RANK, MLA_KV_RANK, LANES)
_W_OFFS = tuple(int(v) for v in np.cumsum((0,) + _W_SPLITS))


def _mixer_in_kernel(x_ref, ln_ref, wcat_ref, wqb_ref, wkb_ref, wvb_ref, gq_ref, gk_ref, gql_ref, gkv_ref,
                     cg_ref, sg_ref, cmq_ref, smq_ref, cmk_ref, smk_ref, oneg_ref, onem_ref,
                     q_ref, k_ref, v_ref, *, sub):
    o = _W_OFFS
    n_vg = GQA_KV_HEADS * LANES
    for r in range(x_ref.shape[1] // sub):
        rows = slice(r * sub, (r + 1) * sub)
        h = (_rms(x_ref[0, rows, :]) * ln_ref[...]).astype(BF16)
        proj = _dot(h, wcat_ref[...])
        cg, sg = cg_ref[rows, :], sg_ref[rows, :]

        def head_norm_rope(blk, gain):
            ss = jnp.sum(blk * blk, axis=-1, keepdims=True) * (1.0 / GQA_HEAD_DIM)
            y = blk * lax.rsqrt(ss + EPS) * gain
            return y * cg + pltpu.roll(y, 64, 1) * sg

        for j in range(GQA_HEADS):
            blk = proj[:, o[0] + j * LANES:o[0] + (j + 1) * LANES]
            q_ref[0, rows, j * LANES:(j + 1) * LANES] = head_norm_rope(blk, gq_ref[...]).astype(BF16)
        for j in range(GQA_KV_HEADS):
            blk = proj[:, o[1] + j * LANES:o[1] + (j + 1) * LANES]
            k_ref[0, rows, j * LANES:(j + 1) * LANES] = head_norm_rope(blk, gk_ref[...]).astype(BF16)
        v_ref[0, rows, 0:n_vg] = (proj[:, o[2]:o[3]] + oneg_ref[...]).astype(BF16)

        c_q = (_rms(proj[:, o[3]:o[4]]) * gql_ref[...]).astype(BF16)
        qm = _dot(c_q, wqb_ref[...])
        cmq, smq = cmq_ref[rows, :], smq_ref[rows, :]
        for j in range(MLA_HEADS):
            blk = qm[:, j * LANES:(j + 1) * LANES]
            q_ref[0, rows, (GQA_HEADS + j) * LANES:(GQA_HEADS + j + 1) * LANES] = (
                blk * cmq + pltpu.roll(blk, 64, 1) * smq).astype(BF16)

        c_kv = (_rms(proj[:, o[4]:o[5]]) * gkv_ref[...]).astype(BF16)
        kn = _dot(c_kv, wkb_ref[...])
        vm = _dot(c_kv, wvb_ref[...]) + onem_ref[...]
        kr = proj[:, o[5]:o[6]]
        kr = kr * cmk_ref[rows, :] + pltpu.roll(kr, 64, 1) * smk_ref[rows, :]
        for j in range(MLA_HEADS):
            k_ref[0, rows, (GQA_KV_HEADS + j) * LANES:(GQA_KV_HEADS + j + 1) * LANES] = (
                kn[:, j * LANES:(j + 1) * LANES] + kr).astype(BF16)
        v_ref[0, rows, n_vg:] = vm.astype(BF16)


def _mixer_in(x, lw, tabs, ts, sub):
    B, S, D = x.shape
    nq, nk = N_HEADS * LANES, (GQA_KV_HEADS + MLA_HEADS) * LANES
    nv = nk
    full = lambda a: pl.BlockSpec(a.shape, lambda b, i: (0,) * a.ndim)
    tab = pl.BlockSpec((ts, LANES), lambda b, i: (i, 0))
    consts = (lw["ln_mix"], lw["wcat"], lw["wqb"], lw["wkb"], lw["wvb"], lw["gq"], lw["gk"], lw["gql"], lw["gkv"])
    return pl.pallas_call(
        functools.partial(_mixer_in_kernel, sub=sub),
        grid=(B, S // ts),
        in_specs=[pl.BlockSpec((1, ts, D), lambda b, i: (b, i, 0))] + [full(a) for a in consts]
        + [tab] * 6 + [full(tabs["one_g"]), full(tabs["one_m"])],
        out_specs=[pl.BlockSpec((1, ts, nq), lambda b, i: (b, i, 0)),
                   pl.BlockSpec((1, ts, nk), lambda b, i: (b, i, 0)),
                   pl.BlockSpec((1, ts, nv), lambda b, i: (b, i, 0))],
        out_shape=[jax.ShapeDtypeStruct((B, S, nq), BF16), jax.ShapeDtypeStruct((B, S, nk), BF16),
                   jax.ShapeDtypeStruct((B, S, nv), BF16)],
        compiler_params=_cparams(2),
        name="mixer_in",
    )(x, *consts, tabs["cos_g"], tabs["sin_g"], tabs["cos_mq"], tabs["sin_mq"], tabs["cos_m"], tabs["sin_m"],
      tabs["one_g"], tabs["one_m"])


SCORE_BOUND_MAX = 40.0
BOUND_SLACK = 1.02


def _dot_tn(a, b):
    return lax.dot_general(a, b, (((0,), (0,)), ((), ())), preferred_element_type=F32)


def _attn_kernel(q_ref, ka_ref, kb_ref, va_ref, vb_ref, o_ref, kmax_ref, *, tk):
    tq = q_ref.shape[1]
    n_chunks = ka_ref.shape[1] // tk
    ones8 = jnp.ones((8, LANES), BF16)
    k_refs = (ka_ref, kb_ref)
    v_refs = (va_ref, vb_ref)

    @pl.when(pl.program_id(2) == 0)
    def _():
        for h in range(2):
            kk = k_refs[h][0]
            ksq = _dot_nt(ones8, kk * kk)
            kmax_ref[h] = jnp.broadcast_to(jnp.max(ksq, axis=-1, keepdims=True), (8, LANES))

    qs = (q_ref[0, :, 0:LANES], q_ref[0, :, LANES:2 * LANES])
    bounds = []
    for h in range(2):
        qsq = _dot_nt(ones8, qs[h] * qs[h])[0:1]
        bounds.append(jnp.sqrt(qsq * kmax_ref[h][0:1, 0:1]) * BOUND_SLACK)
    bound_max = jnp.max(jnp.maximum(bounds[0], bounds[1]))

    def finish(accs):
        outs = [a[0:V_DIM] * (1.0 / a[ONE_LANE:ONE_LANE + 1]) for a in accs]
        o_ref[0] = jnp.concatenate(outs, axis=0).T

    def bounded():
        accs = [jnp.zeros((LANES, tq), F32), jnp.zeros((LANES, tq), F32)]
        for c in range(n_chunks):
            for h in range(2):
                ks = k_refs[h][0, c * tk:(c + 1) * tk, :]
                vs = v_refs[h][0, c * tk:(c + 1) * tk, :]
                pt = jnp.exp2(_dot_nt(ks, qs[h]) - bounds[h]).astype(BF16)
                accs[h] = accs[h] + _dot_tn(vs, pt)
        finish(accs)

    def running_max():
        accs = []
        for h in range(2):
            def body(c, carry):
                m, acc = carry
                start = pl.multiple_of(c * tk, tk)
                ks = k_refs[h][0, pl.ds(start, tk), :]
                vs = v_refs[h][0, pl.ds(start, tk), :]
                st = _dot_nt(ks, qs[h])
                m_new = jnp.maximum(m, jnp.max(st, axis=0, keepdims=True))
                pt = jnp.exp2(st - m_new).astype(BF16)
                return m_new, jnp.exp2(m - m_new) * acc + _dot_tn(vs, pt)

            init = (jnp.full((1, tq), -jnp.inf, F32), jnp.zeros((LANES, tq), F32))
            accs.append(lax.fori_loop(0, n_chunks, body, init)[1])
        finish(accs)

    pl.when(bound_max <= SCORE_BOUND_MAX)(bounded)
    pl.when(jnp.logical_not(bound_max <= SCORE_BOUND_MAX))(running_max)


def _attention(q_all, k_all, v_all, tq, tk):
    B, S, _ = q_all.shape
    pairs_per_kv = GQA_GROUP // 2

    def kv_col(p, second):
        return jnp.where(p < GQA_PAIRS, p // pairs_per_kv, GQA_KV_HEADS + 2 * (p - GQA_PAIRS) + second)

    kv_spec = lambda second: pl.BlockSpec((1, S, LANES), lambda b, p, i: (b, 0, kv_col(p, second)))
    return pl.pallas_call(
        functools.partial(_attn_kernel, tk=tk),
        grid=(B, N_PAIRS, S // tq),
        in_specs=[pl.BlockSpec((1, tq, 2 * LANES), lambda b, p, i: (b, i, p)),
                  kv_spec(0), kv_spec(1), kv_spec(0), kv_spec(1)],
        out_specs=pl.BlockSpec((1, tq, LANES), lambda b, p, i: (b, i, p)),
        out_shape=jax.ShapeDtypeStruct((B, S, N_PAIRS * LANES), F32),
        scratch_shapes=[pltpu.VMEM((2, 8, LANES), F32)],
        compiler_params=_cparams(3),
        name="attention",
    )(q_all, k_all, k_all, v_all, v_all)


def _mem_kv_kernel(m_ref, ln_ref, w_ref, o_ref):
    h = (_rms(m_ref[0]) * ln_ref[...]).astype(BF16)
    o_ref[0] = _dot(h, w_ref[...]).astype(BF16)


def _mem_kv(mem, ln, w):
    B, M, D = mem.shape
    n = w.shape[1]
    return pl.pallas_call(
        _mem_kv_kernel,
        grid=(B,),
        in_specs=[pl.BlockSpec((1, M, D), lambda b: (b, 0, 0)), pl.BlockSpec(ln.shape, lambda b: (0, 0)),
                  pl.BlockSpec(w.shape, lambda b: (0, 0))],
        out_specs=pl.BlockSpec((1, M, n), lambda b: (b, 0, 0)),
        out_shape=jax.ShapeDtypeStruct((B, M, n), BF16),
        compiler_params=_cparams(1),
        name="mem_kv",
    )(mem, ln, w)


def _post_attn_kernel(o_ref, x_ref, on_ref, wo_ref, lnm_ref, wmq_ref, kv_ref, wmo_ref, lnf_ref, wr_ref,
                      x2_ref, h3_ref, aff_ref, *, sub):
    half = o_ref.shape[2] // 2
    n_mem = MEM_HEADS * MEM_HEAD_DIM
    kv = kv_ref[0]
    for r in range(x_ref.shape[1] // sub):
        rows = slice(r * sub, (r + 1) * sub)
        o = o_ref[0, rows, :]
        merged = (jnp.concatenate([_rms(o[:, :half]), _rms(o[:, half:])], axis=-1) * on_ref[...]).astype(BF16)
        x1 = x_ref[0, rows, :] + _dot(merged, wo_ref[...])

        h2 = (_rms(x1) * lnm_ref[...]).astype(BF16)
        q = (_dot(h2, wmq_ref[...]) * (MEM_HEAD_DIM ** -0.5 * LOG2E)).astype(BF16)
        outs = []
        for hh in range(MEM_HEADS):
            lo, hi = hh * MEM_HEAD_DIM, (hh + 1) * MEM_HEAD_DIM
            s = _dot_nt(q[:, lo:hi], kv[:, lo:hi])
            p = jnp.exp2(s - jnp.max(s, axis=-1, keepdims=True))
            l = jnp.sum(p, axis=-1, keepdims=True)
            outs.append(_dot(p.astype(BF16), kv[:, n_mem + lo:n_mem + hi]) * (1.0 / l))
        oc = jnp.concatenate(outs, axis=-1).astype(BF16)
        x2 = x1 + _dot(oc, wmo_ref[...])
        x2_ref[0, rows, :] = x2

        h3 = _rms(x2) * lnf_ref[...]
        h3_ref[0, rows, :] = h3.astype(BF16)
        h_hi = h3.astype(BF16)
        h_lo = (h3 - h_hi.astype(F32)).astype(BF16)
        n_e = wr_ref.shape[0] // 2
        part = _dot_nt(wr_ref[...], h_hi)
        logits = part[:n_e] + part[n_e:] + _dot_nt(wr_ref[0:n_e, :], h_lo)
        e = jnp.exp(logits - jnp.max(logits, axis=0, keepdims=True))
        aff_ref[0, :, rows] = e * (1.0 / jnp.sum(e, axis=0, keepdims=True))


def _post_attn(o, x, kv_mem, lw, ts, sub):
    B, S, D = x.shape
    E = N_EXPERTS
    consts_a = (lw["on"], lw["w_o"], lw["ln_mem"], lw["w_mem_q"])
    consts_b = (lw["w_mem_o"], lw["ln_ffn"], lw["w_router_t"])
    full = lambda a: pl.BlockSpec(a.shape, lambda b, i: (0,) * a.ndim)
    tile = lambda n: pl.BlockSpec((1, ts, n), lambda b, i: (b, i, 0))
    return pl.pallas_call(
        functools.partial(_post_attn_kernel, sub=sub),
        grid=(B, S // ts),
        in_specs=[tile(o.shape[2]), tile(D)] + [full(a) for a in consts_a]
        + [pl.BlockSpec((1,) + kv_mem.shape[1:], lambda b, i: (b, 0, 0))] + [full(a) for a in consts_b],
        out_specs=[tile(D), tile(D), pl.BlockSpec((1, E, ts), lambda b, i: (b, 0, i))],
        out_shape=[jax.ShapeDtypeStruct((B, S, D), F32), jax.ShapeDtypeStruct((B, S, D), BF16),
                   jax.ShapeDtypeStruct((B, E, S), F32)],
        compiler_params=_cparams(2),
        name="post_attn",
    )(o, x, *consts_a, kv_mem, *consts_b)


def _select_kernel(aff_ref, slot_ref, pos_ref, *, cap, chunks, seq_rows):
    a = aff_ref[...]
    n_rows = a.shape[0]
    bits = pltpu.bitcast(a, jnp.int32)

    r_i = lax.broadcasted_iota(jnp.int32, (seq_rows, seq_rows), 0)
    c_i = lax.broadcasted_iota(jnp.int32, (seq_rows, seq_rows), 1)
    earlier_chunk = ((r_i // chunks) == (c_i // chunks)) & (c_i < r_i)
    bd_before = jnp.where(earlier_chunk, 1.0, 0.0).astype(BF16)
    l_r = lax.broadcasted_iota(jnp.int32, (LANES, LANES), 0)
    l_c = lax.broadcasted_iota(jnp.int32, (LANES, LANES), 1)
    ones = jnp.ones((LANES, LANES), BF16)
    before = jnp.where(l_r < l_c, 1.0, 0.0).astype(BF16)

    def as01(mask):
        return jnp.where(mask, 1.0, 0.0).astype(BF16)

    def expert_count(x01):
        per_chunk = _dot(x01, ones).reshape(n_rows // chunks, chunks, LANES)
        total = jnp.sum(per_chunk, axis=1, keepdims=True)
        return jnp.broadcast_to(total, per_chunk.shape).reshape(n_rows, LANES)

    def prefix_excl(x01):
        per_chunk = _dot(x01, ones).astype(BF16)
        earlier = [_dot(bd_before, per_chunk[s * seq_rows:(s + 1) * seq_rows]) for s in range(n_rows // seq_rows)]
        return _dot(x01, before) + jnp.concatenate(earlier, axis=0)

    def step(i, theta):
        cand = theta | (jnp.int32(1) << (30 - i))
        cnt = expert_count(as01(bits >= cand))
        return jnp.where(cnt >= cap, cand, theta)

    theta = lax.fori_loop(0, 31, step, jnp.zeros(bits.shape, jnp.int32))
    gt = bits > theta
    eq = bits == theta
    need = cap - expert_count(as01(gt))
    sel = gt | (eq & (prefix_excl(as01(eq)) < need))
    pos = prefix_excl(as01(sel))
    slot_ref[...] = jnp.where(sel, pos, -1.0).astype(jnp.int32)
    pos_ref[...] = pos.astype(jnp.int32)


def _select(aff2, cap, chunks):
    B, seq_rows, _ = aff2.shape
    assert chunks % 8 == 0, "the per-expert reduction reshapes rows into whole sublane tiles"
    n_rows = B * seq_rows
    blk = pl.BlockSpec((n_rows, LANES), lambda i: (0, 0))
    slot, pos = pl.pallas_call(
        functools.partial(_select_kernel, cap=cap, chunks=chunks, seq_rows=seq_rows),
        grid=(1,),
        in_specs=[blk],
        out_specs=[blk, blk],
        out_shape=[jax.ShapeDtypeStruct((n_rows, LANES), jnp.int32)] * 2,
        compiler_params=_cparams(1),
        name="select",
    )(aff2.reshape(n_rows, LANES))
    return slot.reshape(aff2.shape), pos.reshape(aff2.shape)


GATHER_TILE = 2 * LANES
GATHER_WIN = GATHER_TILE + 8
SCATTER_TILE = LANES
SCATTER_WIN = 2 * LANES


def _y_rows(cap):
    return -(-cap // LANES) * LANES + LANES


def _expert_kernel(ps_ref, slot_ref, aff_ref, h_ref, wg_ref, wu_ref, wd_ref, y_ref, xin_ref, gate_ref, *, cap, chunks):
    base = (pl.program_id(0) * N_EXPERTS + pl.program_id(1)) * chunks
    S = h_ref.shape[1]
    xin_ref[...] = jnp.zeros_like(xin_ref)
    gate_ref[...] = jnp.zeros_like(gate_ref)
    row = lax.broadcasted_iota(jnp.int32, (GATHER_WIN, GATHER_TILE), 0)
    for t in range(S // GATHER_TILE):
        lo, hi = t * GATHER_TILE, (t + 1) * GATHER_TILE
        start = pl.multiple_of((ps_ref[base + t * (GATHER_TILE // LANES)] >> 3) << 3, 8)
        hit = (row + start) == slot_ref[0, 0, :, lo:hi]
        onehot = jnp.where(hit, 1.0, 0.0).astype(BF16)
        xin_ref[pl.ds(start, GATHER_WIN), :] += _dot(onehot, h_ref[0, lo:hi, :])
        g = jnp.sum(jnp.where(hit, aff_ref[0, 0, :, lo:hi], 0.0), axis=-1, keepdims=True)
        gate_ref[pl.ds(start, GATHER_WIN), :] += jnp.broadcast_to(g, (GATHER_WIN, LANES))
    x_in = xin_ref[0:cap, :].astype(BF16)
    a = _dot(x_in, wg_ref[0])
    u = _dot(x_in, wu_ref[0])
    hm = (a * (1.0 / (1.0 + jnp.exp(-a))) * u).astype(BF16)
    y_ref[0, 0, 0:cap, :] = (_dot(hm, wd_ref[0]) * gate_ref[0:cap, 0:1]).astype(BF16)
    y_ref[0, 0, cap:, :] = jnp.zeros((y_ref.shape[2] - cap, y_ref.shape[3]), BF16)


def _experts(pstart, slot_row, aff_row, h3, wg, wu, wd, cap, chunks):
    B, S, D = h3.shape
    E = N_EXPERTS
    F = wg.shape[2]
    assert S % GATHER_TILE == 0 and cap % 8 == 0
    row = pl.BlockSpec((1, 1, 1, S), lambda b, e, ps: (b, e, 0, 0))
    return pl.pallas_call(
        functools.partial(_expert_kernel, cap=cap, chunks=chunks),
        grid_spec=pltpu.PrefetchScalarGridSpec(
            num_scalar_prefetch=1,
            grid=(B, E),
            in_specs=[row, row, pl.BlockSpec((1, S, D), lambda b, e, ps: (b, 0, 0)),
                      pl.BlockSpec((1, D, F), lambda b, e, ps: (e, 0, 0)),
                      pl.BlockSpec((1, D, F), lambda b, e, ps: (e, 0, 0)),
                      pl.BlockSpec((1, F, D), lambda b, e, ps: (e, 0, 0))],
            out_specs=pl.BlockSpec((1, 1, _y_rows(cap), D), lambda b, e, ps: (b, e, 0, 0)),
            scratch_shapes=[pltpu.VMEM((cap + GATHER_WIN, D), F32), pltpu.VMEM((cap + GATHER_WIN, LANES), F32)]),
        out_shape=jax.ShapeDtypeStruct((B, E, _y_rows(cap), D), BF16),
        compiler_params=_cparams(2),
        name="experts",
    )(pstart, slot_row, aff_row, h3, wg, wu, wd)


def _combine_kernel(ps_ref, x_ref, slot_ref, y_ref, lnf_ref, o_ref, *, chunks, final):
    tc = x_ref.shape[1]
    b, i = pl.program_id(0), pl.program_id(1)
    col = lax.broadcasted_iota(jnp.int32, (SCATTER_TILE, SCATTER_WIN), 1)
    for sub in range(tc // SCATTER_TILE):
        lo, hi = sub * SCATTER_TILE, (sub + 1) * SCATTER_TILE
        chunk = i * (tc // SCATTER_TILE) + sub
        slot_t = slot_ref[0, lo:hi, :]
        hots, wins = [], []
        for e in range(N_EXPERTS):
            first = (ps_ref[(b * N_EXPERTS + e) * chunks + chunk] >> 7) << 7
            start = pl.multiple_of(jnp.minimum(first, y_ref.shape[2] - SCATTER_WIN), LANES)
            hots.append(jnp.where(slot_t[:, e:e + 1] == col + start, 1.0, 0.0).astype(BF16))
            wins.append(y_ref[0, e, pl.ds(start, SCATTER_WIN), :])
        acc = x_ref[0, lo:hi, :] + _dot(jnp.concatenate(hots, axis=1), jnp.concatenate(wins, axis=0))
        if final:
            acc = _rms(acc) * lnf_ref[...]
        o_ref[0, lo:hi, :] = acc


def _combine(pstart, x2, slot_t, y, ln_final, chunks, tc, final):
    B, S, D = x2.shape
    E = N_EXPERTS
    assert tc % SCATTER_TILE == 0 and SCATTER_TILE == LANES
    return pl.pallas_call(
        functools.partial(_combine_kernel, chunks=chunks, final=final),
        grid_spec=pltpu.PrefetchScalarGridSpec(
            num_scalar_prefetch=1,
            grid=(B, S // tc),
            in_specs=[pl.BlockSpec((1, tc, D), lambda b, i, ps: (b, i, 0)),
                      pl.BlockSpec((1, tc, E), lambda b, i, ps: (b, i, 0)),
                      pl.BlockSpec((1,) + y.shape[1:], lambda b, i, ps: (b, 0, 0, 0), pipeline_mode=pl.Buffered(1)),
                      pl.BlockSpec(ln_final.shape, lambda b, i, ps: (0, 0))],
            out_specs=pl.BlockSpec((1, tc, D), lambda b, i, ps: (b, i, 0))),
        out_shape=jax.ShapeDtypeStruct((B, S, D), F32),
        compiler_params=_cparams(2),
        name="combine",
    )(pstart, x2, slot_t, y, ln_final)


def _layer_weights(l, p):
    d_model = p["w_in"].shape[1]
    w_in = p["w_in"][l]
    o1 = GQA_HEADS * GQA_HEAD_DIM
    o2 = o1 + GQA_KV_HEADS * GQA_HEAD_DIM
    o3 = o2 + GQA_KV_HEADS * GQA_HEAD_DIM
    o4 = o3 + MLA_Q_RANK
    o5 = o4 + MLA_KV_RANK
    heads = lambda w, n: w.reshape(w.shape[0], n, w.shape[1] // n)
    flat = lambda w: w.reshape(w.shape[0], -1)
    wq = flat(_lay_gqa(heads(w_in[:, :o1], GQA_HEADS)))
    wk = flat(_lay_gqa(heads(w_in[:, o1:o2], GQA_KV_HEADS)))
    wv = heads(w_in[:, o2:o3], GQA_KV_HEADS)
    wv = flat(_lay_v(wv))
    w_kr = w_in[:, o5:]
    wkr = _lay_mla(jnp.zeros((d_model, MLA_NOPE_DIM), F32), w_kr)
    wcat = jnp.concatenate([wq, wk, wv, w_in[:, o3:o4], w_in[:, o4:o5], wkr], axis=1).astype(BF16)

    wqb = heads(p["w_q_b"][l], MLA_HEADS)
    wqb = flat(_lay_mla(wqb[..., :MLA_NOPE_DIM], wqb[..., MLA_NOPE_DIM:])).astype(BF16)
    wkvb = heads(p["w_kv_b"][l], MLA_HEADS)
    k_nope = wkvb[..., :MLA_NOPE_DIM]
    wkb = flat(_lay_mla(k_nope, jnp.zeros(k_nope.shape[:-1] + (MLA_ROPE_DIM,), F32))).astype(BF16)
    wvb = flat(_lay_v(wkvb[..., MLA_NOPE_DIM:])).astype(BF16)

    q_scale = GQA_HEAD_DIM ** -0.5 * LOG2E
    gq = _lay_gqa(p["gqa_q_norm"][l] * q_scale).reshape(1, LANES)
    gk = _lay_gqa(p["gqa_k_norm"][l]).reshape(1, LANES)
    row = lambda v: v.reshape(1, -1)
    return {
        "ln_mix": row(p["ln_mix"][l]), "wcat": wcat, "wqb": wqb, "wkb": wkb, "wvb": wvb, "gq": gq, "gk": gk,
        "gql": row(p["mla_q_norm"][l]), "gkv": row(p["mla_kv_norm"][l]),
        "on": jnp.concatenate([p["out_norm_gqa"][l], p["out_norm_mla"][l]]).reshape(1, -1),
        "w_o": p["w_o"][l].astype(BF16), "ln_mem": row(p["ln_mem"][l]), "ln_mem_kv": row(p["ln_mem_kv"][l]),
        "w_mem_q": p["w_mem_q"][l].astype(BF16), "w_mem_kv": p["w_mem_kv"][l].astype(BF16),
        "w_mem_o": p["w_mem_o"][l].astype(BF16), "ln_ffn": row(p["ln_ffn"][l]),
        "w_router_t": _split_hi_lo(p["w_router"][l].T),
        "w_gate": p["w_gate"][l].astype(BF16), "w_up": p["w_up"][l].astype(BF16),
        "w_down": p["w_down"][l].astype(BF16),
    }


def _split_hi_lo(w):
    hi = w.astype(BF16)
    lo = (w - hi.astype(F32)).astype(BF16)
    return jnp.concatenate([hi, lo], axis=0)


def _tables(seq_len):
    cos_g, sin_g, cos_m, sin_m = _rope_tables(seq_len)
    mq_scale = MLA_QK_DIM ** -0.5 * LOG2E
    one_g = jnp.zeros((GQA_KV_HEADS, LANES), F32).at[:, ONE_LANE].set(1.0)
    one_m = jnp.zeros((MLA_HEADS, LANES), F32).at[:, ONE_LANE].set(1.0)
    return {"cos_g": cos_g, "sin_g": sin_g, "cos_m": cos_m, "sin_m": sin_m,
            "cos_mq": cos_m * mq_scale, "sin_mq": sin_m * mq_scale,
            "one_g": one_g.reshape(1, -1), "one_m": one_m.reshape(1, -1)}


def _pick(n, pref):
    t = min(n, pref)
    assert n % t == 0, (n, t)
    return t


def kernel(x, mem, ln_mix, w_in, gqa_q_norm, gqa_k_norm, mla_q_norm, mla_kv_norm, w_q_b, w_kv_b, out_norm_gqa,
           out_norm_mla, w_o, ln_mem, ln_mem_kv, w_mem_q, w_mem_kv, w_mem_o, ln_ffn, w_router, w_gate, w_up,
           w_down, ln_final):
    p = dict(ln_mix=ln_mix, w_in=w_in, gqa_q_norm=gqa_q_norm, gqa_k_norm=gqa_k_norm, mla_q_norm=mla_q_norm,
             mla_kv_norm=mla_kv_norm, w_q_b=w_q_b, w_kv_b=w_kv_b, out_norm_gqa=out_norm_gqa,
             out_norm_mla=out_norm_mla, w_o=w_o, ln_mem=ln_mem, ln_mem_kv=ln_mem_kv, w_mem_q=w_mem_q,
             w_mem_kv=w_mem_kv, w_mem_o=w_mem_o, ln_ffn=ln_ffn, w_router=w_router, w_gate=w_gate, w_up=w_up,
             w_down=w_down)
    B, S, D = x.shape
    depth = w_in.shape[0]
    E = N_EXPERTS
    assert S % LANES == 0 and S % GRID_W == 0
    cap = EC_CAPACITY_FACTOR * S // E
    chunks = S // LANES
    tabs = _tables(S)
    ln_final2 = ln_final.reshape(1, -1)
    ts_in, ts_post, tq, tk, tc = _pick(S, 256), _pick(S, 1024), _pick(S, 1024), _pick(S, 512), _pick(S, 256)
    sub_in, sub_post = ts_in, ts_post

    for l in range(depth):
        lw = _layer_weights(l, p)
        q_all, k_all, v_all = _mixer_in(x, lw, tabs, ts_in, sub_in)
        o = _attention(q_all, k_all, v_all, tq, tk)
        kv_mem = _mem_kv(mem, lw["ln_mem_kv"], lw["w_mem_kv"])
        x2, h3, aff = _post_attn(o, x, kv_mem, lw, ts_post, sub_post)
        slot, pos = _select(aff.reshape(B, E * chunks, LANES), cap, chunks)
        slot = slot.reshape(B, E, S)
        pstart = pos[:, :, 0].reshape(B * E * chunks)
        y = _experts(pstart, slot.reshape(B, E, 1, S), aff.reshape(B, E, 1, S), h3, lw["w_gate"], lw["w_up"],
                     lw["w_down"], cap, chunks)
        x = _combine(pstart, x2, jnp.swapaxes(slot, 1, 2), y, ln_final2, chunks, tc, final=(l == depth - 1))
    return x
```

```python
import functools
import math

import numpy as np
import jax
import jax.numpy as jnp
from jax import lax
from jax.experimental import pallas as pl
from jax.experimental.pallas import tpu as pltpu

F32 = jnp.float32
BF16 = jnp.bfloat16

GRID_W = 64
ROPE_THETA = 10000.0
EPS = 1e-6
GQA_HEADS = 8
GQA_KV_HEADS = 2
GQA_GROUP = GQA_HEADS // GQA_KV_HEADS
GQA_HEAD_DIM = 64
MLA_HEADS = 8
MLA_Q_RANK = 256
MLA_KV_RANK = 128
MLA_NOPE_DIM = 64
MLA_ROPE_DIM = 32
MLA_V_DIM = 64
MLA_QK_DIM = MLA_NOPE_DIM + MLA_ROPE_DIM
MEM_HEADS = 4
MEM_HEAD_DIM = 128
N_EXPERTS = 16
EC_CAPACITY_FACTOR = 2

LANES = 128
LOG2E = math.log2(math.e)
VMEM_LIMIT = 56 * 1024 * 1024

N_HEADS = GQA_HEADS + MLA_HEADS
N_PAIRS = N_HEADS // 2
GQA_PAIRS = GQA_HEADS // 2


def _cparams(n_axes):
    return pltpu.CompilerParams(dimension_semantics=("arbitrary",) * n_axes, vmem_limit_bytes=VMEM_LIMIT)


def _rms(x, eps=EPS):
    return x * lax.rsqrt(jnp.mean(x * x, axis=-1, keepdims=True) + eps)


def _dot(a, b):
    return jnp.dot(a, b, preferred_element_type=F32)


def _dot_nt(a, b):
    return lax.dot_general(a, b, (((1,), (1,)), ((), ())), preferred_element_type=F32)


V_DIM = 64
ONE_LANE = V_DIM


def _zeros_like_cols(w, n):
    return jnp.zeros(w.shape[:-1] + (n,), w.dtype)


def _lay_gqa(w):
    z = _zeros_like_cols(w, 32)
    return jnp.concatenate([w[..., :32], z, w[..., 32:], z], axis=-1)


def _lay_mla(nope, rope):
    z = _zeros_like_cols(nope, 16)
    return jnp.concatenate([nope[..., :32], rope[..., :16], z, nope[..., 32:], rope[..., 16:], z], axis=-1)


def _lay_v(v):
    return jnp.concatenate([v, _zeros_like_cols(v, LANES - V_DIM)], axis=-1)


def _rope_tables(seq_len):
    rows = seq_len // GRID_W
    row = jnp.repeat(jnp.arange(rows, dtype=F32), GRID_W)
    col = jnp.tile(jnp.arange(GRID_W, dtype=F32), rows)

    def angles(rot_dim):
        axis_dim = rot_dim // 2
        inv_freq = ROPE_THETA ** (-jnp.arange(0, axis_dim, 2, dtype=F32) / axis_dim)
        ang = jnp.concatenate([row[:, None] * inv_freq[None, :], col[:, None] * inv_freq[None, :]], axis=-1)
        return jnp.cos(ang), jnp.sin(ang)

    cg, sg = angles(GQA_HEAD_DIM)
    cm, sm = angles(MLA_ROPE_DIM)
    cos_g = _lay_gqa(jnp.concatenate([cg, cg], axis=-1))
    sin_g = _lay_gqa(jnp.concatenate([-sg, sg], axis=-1))
    one = jnp.ones((seq_len, MLA_NOPE_DIM), F32)
    cos_m = _lay_mla(one, jnp.concatenate([cm, cm], axis=-1))
    sin_m = _lay_mla(0.0 * one, jnp.concatenate([-sm, sm], axis=-1))
    return cos_g, sin_g, cos_m, sin_m


_W_SPLITS = (GQA_HEADS * LANES, GQA_KV_HEADS * LANES, GQA_KV_HEADS * LANES, MLA_Q_RANK, MLA_KV_RANK, LANES)
_W_OFFS = tuple(int(v) for v in np.cumsum((0,) + _W_SPLITS))


def _mixer_in_kernel(x_ref, ln_ref, wcat_ref, wqb_ref, wkb_ref, wvb_ref, gq_ref, gk_ref, gql_ref, gkv_ref,
                     cg_ref, sg_ref, cmq_ref, smq_ref, cmk_ref, smk_ref, oneg_ref, onem_ref,
                     q_ref, k_ref, v_ref):
    o = _W_OFFS
    n_vg = GQA_KV_HEADS * LANES
    h = (_rms(x_ref[0]) * ln_ref[0]).astype(BF16)
    proj = _dot(h, wcat_ref[0])
    cg, sg = cg_ref[...], sg_ref[...]

    def head_norm_rope(blk, gain):
        ss = jnp.sum(blk * blk, axis=-1, keepdims=True) * (1.0 / GQA_HEAD_DIM)
        y = blk * lax.rsqrt(ss + EPS) * gain
        return y * cg + pltpu.roll(y, 64, 1) * sg

    for j in range(GQA_HEADS):
        blk = proj[:, o[0] + j * LANES:o[0] + (j + 1) * LANES]
        q_ref[0, :, j * LANES:(j + 1) * LANES] = head_norm_rope(blk, gq_ref[0]).astype(BF16)
    for j in range(GQA_KV_HEADS):
        blk = proj[:, o[1] + j * LANES:o[1] + (j + 1) * LANES]
        k_ref[0, :, j * LANES:(j + 1) * LANES] = head_norm_rope(blk, gk_ref[0]).astype(BF16)
    v_ref[0, :, 0:n_vg] = (proj[:, o[2]:o[3]] + oneg_ref[...]).astype(BF16)

    c_q = (_rms(proj[:, o[3]:o[4]]) * gql_ref[0]).astype(BF16)
    qm = _dot(c_q, wqb_ref[0])
    cmq, smq = cmq_ref[...], smq_ref[...]
    for j in range(MLA_HEADS):
        blk = qm[:, j * LANES:(j + 1) * LANES]
        q_ref[0, :, (GQA_HEADS + j) * LANES:(GQA_HEADS + j + 1) * LANES] = (
            blk * cmq + pltpu.roll(blk, 64, 1) * smq).astype(BF16)

    c_kv = (_rms(proj[:, o[4]:o[5]]) * gkv_ref[0]).astype(BF16)
    kn = _dot(c_kv, wkb_ref[0])
    vm = _dot(c_kv, wvb_ref[0]) + onem_ref[...]
    kr = proj[:, o[5]:o[6]]
    kr = kr * cmk_ref[...] + pltpu.roll(kr, 64, 1) * smk_ref[...]
    for j in range(MLA_HEADS):
        k_ref[0, :, (GQA_KV_HEADS + j) * LANES:(GQA_KV_HEADS + j + 1) * LANES] = (
            kn[:, j * LANES:(j + 1) * LANES] + kr).astype(BF16)
    v_ref[0, :, n_vg:] = vm.astype(BF16)


def _layer_spec(a, l):
    zeros = (0,) * (a.ndim - 1)
    return pl.BlockSpec((1,) + a.shape[1:], lambda *grid_ids: (l,) + zeros)


def _mixer_in(x, sw, l, tabs, ts):
    B, S, D = x.shape
    nq, nk = N_HEADS * LANES, (GQA_KV_HEADS + MLA_HEADS) * LANES
    nv = nk
    full = lambda a: pl.BlockSpec(a.shape, lambda b, i: (0,) * a.ndim)
    tab = pl.BlockSpec((ts, LANES), lambda b, i: (i, 0))
    consts = (sw["ln_mix"], sw["wcat"], sw["wqb"], sw["wkb"], sw["wvb"], sw["gq"], sw["gk"], sw["gql"], sw["gkv"])
    return pl.pallas_call(
        _mixer_in_kernel,
        grid=(B, S // ts),
        in_specs=[pl.BlockSpec((1, ts, D), lambda b, i: (b, i, 0))] + [_layer_spec(a, l) for a in consts]
        + [tab] * 6 + [full(tabs["one_g"]), full(tabs["one_m"])],
        out_specs=[pl.BlockSpec((1, ts, nq), lambda b, i: (b, i, 0)),
                   pl.BlockSpec((1, ts, nk), lambda b, i: (b, i, 0)),
                   pl.BlockSpec((1, ts, nv), lambda b, i: (b, i, 0))],
        out_shape=[jax.ShapeDtypeStruct((B, S, nq), BF16), jax.ShapeDtypeStruct((B, S, nk), BF16),
                   jax.ShapeDtypeStruct((B, S, nv), BF16)],
        compiler_params=_cparams(2),
        name="mixer_in",
    )(x, *consts, tabs["cos_g"], tabs["sin_g"], tabs["cos_mq"], tabs["sin_mq"], tabs["cos_m"], tabs["sin_m"],
      tabs["one_g"], tabs["one_m"])


SCORE_BOUND_MAX = 40.0
BOUND_SLACK = 1.02


def _dot_tn(a, b):
    return lax.dot_general(a, b, (((0,), (0,)), ((), ())), preferred_element_type=F32)


def _attn_kernel(q_ref, ka_ref, kb_ref, va_ref, vb_ref, o_ref, kmax_ref, *, tk):
    tq = q_ref.shape[1]
    n_chunks = ka_ref.shape[1] // tk
    ones8 = jnp.ones((8, LANES), BF16)
    k_refs = (ka_ref, kb_ref)
    v_refs = (va_ref, vb_ref)

    @pl.when(pl.program_id(2) == 0)
    def _():
        for h in range(2):
            kk = k_refs[h][0]
            ksq = _dot_nt(ones8, kk * kk)
            kmax_ref[h] = jnp.broadcast_to(jnp.max(ksq, axis=-1, keepdims=True), (8, LANES))

    qs = (q_ref[0, :, 0:LANES], q_ref[0, :, LANES:2 * LANES])
    bounds = []
    for h in range(2):
        qsq = _dot_nt(ones8, qs[h] * qs[h])[0:1]
        bounds.append(jnp.sqrt(qsq * kmax_ref[h][0:1, 0:1]) * BOUND_SLACK)
    bound_max = jnp.max(jnp.maximum(bounds[0], bounds[1]))

    def finish(accs):
        outs = [a[0:V_DIM] * (1.0 / a[ONE_LANE:ONE_LANE + 1]) for a in accs]
        o_ref[0] = jnp.concatenate(outs, axis=0).T

    def bounded():
        accs = [jnp.zeros((LANES, tq), F32), jnp.zeros((LANES, tq), F32)]
        for c in range(n_chunks):
            for h in range(2):
                ks = k_refs[h][0, c * tk:(c + 1) * tk, :]
                vs = v_refs[h][0, c * tk:(c + 1) * tk, :]
                pt = jnp.exp2(_dot_nt(ks, qs[h]) - bounds[h]).astype(BF16)
                accs[h] = accs[h] + _dot_tn(vs, pt)
        finish(accs)

    def running_max():
        accs = []
        for h in range(2):
            def body(c, carry):
                m, acc = carry
                start = pl.multiple_of(c * tk, tk)
                ks = k_refs[h][0, pl.ds(start, tk), :]
                vs = v_refs[h][0, pl.ds(start, tk), :]
                st = _dot_nt(ks, qs[h])
                m_new = jnp.maximum(m, jnp.max(st, axis=0, keepdims=True))
                pt = jnp.exp2(st - m_new).astype(BF16)
                return m_new, jnp.exp2(m - m_new) * acc + _dot_tn(vs, pt)

            init = (jnp.full((1, tq), -jnp.inf, F32), jnp.zeros((LANES, tq), F32))
            accs.append(lax.fori_loop(0, n_chunks, body, init)[1])
        finish(accs)

    pl.when(bound_max <= SCORE_BOUND_MAX)(bounded)
    pl.when(jnp.logical_not(bound_max <= SCORE_BOUND_MAX))(running_max)


def _attention(q_all, k_all, v_all, tq, tk):
    B, S, _ = q_all.shape
    pairs_per_kv = GQA_GROUP // 2

    def kv_col(p, second):
        return jnp.where(p < GQA_PAIRS, p // pairs_per_kv, GQA_KV_HEADS + 2 * (p - GQA_PAIRS) + second)

    kv_spec = lambda second: pl.BlockSpec((1, S, LANES), lambda b, p, i: (b, 0, kv_col(p, second)))
    return pl.pallas_call(
        functools.partial(_attn_kernel, tk=tk),
        grid=(B, N_PAIRS, S // tq),
        in_specs=[pl.BlockSpec((1, tq, 2 * LANES), lambda b, p, i: (b, i, p)),
                  kv_spec(0), kv_spec(1), kv_spec(0), kv_spec(1)],
        out_specs=pl.BlockSpec((1, tq, LANES), lambda b, p, i: (b, i, p)),
        out_shape=jax.ShapeDtypeStruct((B, S, N_PAIRS * LANES), F32),
        scratch_shapes=[pltpu.VMEM((2, 8, LANES), F32)],
        compiler_params=_cparams(3),
        name="attention",
    )(q_all, k_all, k_all, v_all, v_all)


def _mem_kv_kernel(m_ref, ln_ref, w_ref, o_ref):
    h = (_rms(m_ref[0]) * ln_ref[0]).astype(BF16)
    o_ref[0] = _dot(h, w_ref[0]).astype(BF16)


def _mem_kv(mem, sw, l):
    B, M, D = mem.shape
    ln, w = sw["ln_mem_kv"], sw["w_mem_kv"]
    n = w.shape[2]
    return pl.pallas_call(
        _mem_kv_kernel,
        grid=(B,),
        in_specs=[pl.BlockSpec((1, M, D), lambda b: (b, 0, 0)), _layer_spec(ln, l), _layer_spec(w, l)],
        out_specs=pl.BlockSpec((1, M, n), lambda b: (b, 0, 0)),
        out_shape=jax.ShapeDtypeStruct((B, M, n), BF16),
        compiler_params=_cparams(1),
        name="mem_kv",
    )(mem, ln, w)


def _post_attn_kernel(o_ref, x_ref, on_ref, wo_ref, lnm_ref, wmq_ref, kv_ref, wmo_ref, lnf_ref, wr_ref,
                      x2_ref, h3_ref, aff_ref):
    o = o_ref[0]
    half = o.shape[1] // 2
    merged = (jnp.concatenate([_rms(o[:, :half]), _rms(o[:, half:])], axis=-1) * on_ref[0]).astype(BF16)
    x1 = x_ref[0] + _dot(merged, wo_ref[0])

    h2 = (_rms(x1) * lnm_ref[0]).astype(BF16)
    q = (_dot(h2, wmq_ref[0]) * (MEM_HEAD_DIM ** -0.5 * LOG2E)).astype(BF16)
    kv = kv_ref[0]
    n_mem = MEM_HEADS * MEM_HEAD_DIM
    outs = []
    for hh in range(MEM_HEADS):
        lo, hi = hh * MEM_HEAD_DIM, (hh + 1) * MEM_HEAD_DIM
        s = _dot_nt(q[:, lo:hi], kv[:, lo:hi])
        p = jnp.exp2(s - jnp.max(s, axis=-1, keepdims=True))
        l = jnp.sum(p, axis=-1, keepdims=True)
        outs.append(_dot(p.astype(BF16), kv[:, n_mem + lo:n_mem + hi]) * (1.0 / l))
    oc = jnp.concatenate(outs, axis=-1).astype(BF16)
    x2 = x1 + _dot(oc, wmo_ref[0])
    x2_ref[0] = x2

    h3 = _rms(x2) * lnf_ref[0]
    h3_ref[0] = h3.astype(BF16)
    h_hi = h3.astype(BF16)
    h_lo = (h3 - h_hi.astype(F32)).astype(BF16)
    n_e = wr_ref.shape[1] // 2
    part = _dot_nt(wr_ref[0], h_hi)
    logits = part[:n_e] + part[n_e:] + _dot_nt(wr_ref[0, 0:n_e, :], h_lo)
    e = jnp.exp(logits - jnp.max(logits, axis=0, keepdims=True))
    aff_ref[0] = e * (1.0 / jnp.sum(e, axis=0, keepdims=True))


def _post_attn(o, x, kv_mem, sw, l, ts):
    B, S, D = x.shape
    E = N_EXPERTS
    consts_a = (sw["on"], sw["w_o"], sw["ln_mem"], sw["w_mem_q"])
    consts_b = (sw["w_mem_o"], sw["ln_ffn"], sw["w_router_t"])
    tile = lambda n: pl.BlockSpec((1, ts, n), lambda b, i: (b, i, 0))
    return pl.pallas_call(
        _post_attn_kernel,
        grid=(B, S // ts),
        in_specs=[tile(o.shape[2]), tile(D)] + [_layer_spec(a, l) for a in consts_a]
        + [pl.BlockSpec((1,) + kv_mem.shape[1:], lambda b, i: (b, 0, 0))] + [_layer_spec(a, l) for a in consts_b],
        out_specs=[tile(D), tile(D), pl.BlockSpec((1, E, ts), lambda b, i: (b, 0, i))],
        out_shape=[jax.ShapeDtypeStruct((B, S, D), F32), jax.ShapeDtypeStruct((B, S, D), BF16),
                   jax.ShapeDtypeStruct((B, E, S), F32)],
        compiler_params=_cparams(2),
        name="post_attn",
    )(o, x, *consts_a, kv_mem, *consts_b)


def _select_kernel(aff_ref, slot_ref, pos_ref, *, cap, chunks, seq_rows):
    a = aff_ref[...]
    n_rows = a.shape[0]
    bits = pltpu.bitcast(a, jnp.int32)

    r_i = lax.broadcasted_iota(jnp.int32, (seq_rows, seq_rows), 0)
    c_i = lax.broadcasted_iota(jnp.int32, (seq_rows, seq_rows), 1)
    earlier_chunk = ((r_i // chunks) == (c_i // chunks)) & (c_i < r_i)
    bd_before = jnp.where(earlier_chunk, 1.0, 0.0).astype(BF16)
    l_r = lax.broadcasted_iota(jnp.int32, (LANES, LANES), 0)
    l_c = lax.broadcasted_iota(jnp.int32, (LANES, LANES), 1)
    ones = jnp.ones((LANES, LANES), BF16)
    before = jnp.where(l_r < l_c, 1.0, 0.0).astype(BF16)

    def as01(mask):
        return jnp.where(mask, 1.0, 0.0).astype(BF16)

    def expert_count(x01):
        per_chunk = _dot(x01, ones).reshape(n_rows // chunks, chunks, LANES)
        total = jnp.sum(per_chunk, axis=1, keepdims=True)
        return jnp.broadcast_to(total, per_chunk.shape).reshape(n_rows, LANES)

    def prefix_excl(x01):
        per_chunk = _dot(x01, ones).astype(BF16)
        earlier = [_dot(bd_before, per_chunk[s * seq_rows:(s + 1) * seq_rows]) for s in range(n_rows // seq_rows)]
        return _dot(x01, before) + jnp.concatenate(earlier, axis=0)

    def step(i, theta):
        cand = theta | (jnp.int32(1) << (30 - i))
        cnt = expert_count(as01(bits >= cand))
        return jnp.where(cnt >= cap, cand, theta)

    theta = lax.fori_loop(0, 31, step, jnp.zeros(bits.shape, jnp.int32))
    gt = bits > theta
    eq = bits == theta
    need = cap - expert_count(as01(gt))
    sel = gt | (eq & (prefix_excl(as01(eq)) < need))
    pos = prefix_excl(as01(sel))
    slot_ref[...] = jnp.where(sel, pos, -1.0).astype(jnp.int32)
    pos_ref[...] = pos.astype(jnp.int32)


def _select(aff2, cap, chunks):
    B, seq_rows, _ = aff2.shape
    assert chunks % 8 == 0, "the per-expert reduction reshapes rows into whole sublane tiles"
    n_rows = B * seq_rows
    blk = pl.BlockSpec((n_rows, LANES), lambda i: (0, 0))
    slot, pos = pl.pallas_call(
        functools.partial(_select_kernel, cap=cap, chunks=chunks, seq_rows=seq_rows),
        grid=(1,),
        in_specs=[blk],
        out_specs=[blk, blk],
        out_shape=[jax.ShapeDtypeStruct((n_rows, LANES), jnp.int32)] * 2,
        compiler_params=_cparams(1),
        name="select",
    )(aff2.reshape(n_rows, LANES))
    return slot.reshape(aff2.shape), pos.reshape(aff2.shape)


GATHER_TILE = 2 * LANES
GATHER_WIN = GATHER_TILE + 8
SCATTER_TILE = LANES
SCATTER_WIN = 2 * LANES


def _y_rows(cap):
    return -(-cap // LANES) * LANES + LANES


def _expert_kernel(ps_ref, slot_ref, aff_ref, h_ref, wg_ref, wu_ref, wd_ref, y_ref, xin_ref, gate_ref, *, cap, chunks):
    base = (pl.program_id(0) * N_EXPERTS + pl.program_id(1)) * chunks
    S = h_ref.shape[1]
    xin_ref[...] = jnp.zeros_like(xin_ref)
    gate_ref[...] = jnp.zeros_like(gate_ref)
    row = lax.broadcasted_iota(jnp.int32, (GATHER_WIN, GATHER_TILE), 0)
    for t in range(S // GATHER_TILE):
        lo, hi = t * GATHER_TILE, (t + 1) * GATHER_TILE
        start = pl.multiple_of((ps_ref[base + t * (GATHER_TILE // LANES)] >> 3) << 3, 8)
        hit = (row + start) == slot_ref[0, 0, :, lo:hi]
        onehot = jnp.where(hit, 1.0, 0.0).astype(BF16)
        xin_ref[pl.ds(start, GATHER_WIN), :] += _dot(onehot, h_ref[0, lo:hi, :])
        g = jnp.sum(jnp.where(hit, aff_ref[0, 0, :, lo:hi], 0.0), axis=-1, keepdims=True)
        gate_ref[pl.ds(start, GATHER_WIN), :] += jnp.broadcast_to(g, (GATHER_WIN, LANES))
    x_in = xin_ref[0:cap, :].astype(BF16)
    a = _dot(x_in, wg_ref[0, 0].astype(BF16))
    u = _dot(x_in, wu_ref[0, 0].astype(BF16))
    hm = (a * (1.0 / (1.0 + jnp.exp(-a))) * u).astype(BF16)
    y_ref[0, 0, 0:cap, :] = (_dot(hm, wd_ref[0, 0].astype(BF16)) * gate_ref[0:cap, 0:1]).astype(BF16)
    y_ref[0, 0, cap:, :] = jnp.zeros((y_ref.shape[2] - cap, y_ref.shape[3]), BF16)


def _experts(pstart, slot_row, aff_row, h3, wg, wu, wd, l, cap, chunks):
    B, S, D = h3.shape
    E = N_EXPERTS
    F = wg.shape[3]
    assert S % GATHER_TILE == 0 and cap % 8 == 0
    row = pl.BlockSpec((1, 1, 1, S), lambda b, e, ps: (b, e, 0, 0))
    return pl.pallas_call(
        functools.partial(_expert_kernel, cap=cap, chunks=chunks),
        grid_spec=pltpu.PrefetchScalarGridSpec(
            num_scalar_prefetch=1,
            grid=(B, E),
            in_specs=[row, row, pl.BlockSpec((1, S, D), lambda b, e, ps: (b, 0, 0)),
                      pl.BlockSpec((1, 1, D, F), lambda b, e, ps: (l, e, 0, 0)),
                      pl.BlockSpec((1, 1, D, F), lambda b, e, ps: (l, e, 0, 0)),
                      pl.BlockSpec((1, 1, F, D), lambda b, e, ps: (l, e, 0, 0))],
            out_specs=pl.BlockSpec((1, 1, _y_rows(cap), D), lambda b, e, ps: (b, e, 0, 0)),
            scratch_shapes=[pltpu.VMEM((cap + GATHER_WIN, D), F32), pltpu.VMEM((cap + GATHER_WIN, LANES), F32)]),
        out_shape=jax.ShapeDtypeStruct((B, E, _y_rows(cap), D), BF16),
        compiler_params=_cparams(2),
        name="experts",
    )(pstart, slot_row, aff_row, h3, wg, wu, wd)


def _combine_kernel(ps_ref, x_ref, slot_ref, y_ref, lnf_ref, o_ref, *, chunks, final):
    tc = x_ref.shape[1]
    b, i = pl.program_id(0), pl.program_id(1)
    col = lax.broadcasted_iota(jnp.int32, (SCATTER_TILE, SCATTER_WIN), 1)
    for sub in range(tc // SCATTER_TILE):
        lo, hi = sub * SCATTER_TILE, (sub + 1) * SCATTER_TILE
        chunk = i * (tc // SCATTER_TILE) + sub
        slot_t = slot_ref[0, lo:hi, :]
        hots, wins = [], []
        for e in range(N_EXPERTS):
            first = (ps_ref[(b * N_EXPERTS + e) * chunks + chunk] >> 7) << 7
            start = pl.multiple_of(jnp.minimum(first, y_ref.shape[2] - SCATTER_WIN), LANES)
            hots.append(jnp.where(slot_t[:, e:e + 1] == col + start, 1.0, 0.0).astype(BF16))
            wins.append(y_ref[0, e, pl.ds(start, SCATTER_WIN), :])
        acc = x_ref[0, lo:hi, :] + _dot(jnp.concatenate(hots, axis=1), jnp.concatenate(wins, axis=0))
        if final:
            acc = _rms(acc) * lnf_ref[...]
        o_ref[0, lo:hi, :] = acc


def _combine(pstart, x2, slot_t, y, ln_final, chunks, tc, final):
    B, S, D = x2.shape
    E = N_EXPERTS
    assert tc % SCATTER_TILE == 0 and SCATTER_TILE == LANES
    return pl.pallas_call(
        functools.partial(_combine_kernel, chunks=chunks, final=final),
        grid_spec=pltpu.PrefetchScalarGridSpec(
            num_scalar_prefetch=1,
            grid=(B, S // tc),
            in_specs=[pl.BlockSpec((1, tc, D), lambda b, i, ps: (b, i, 0)),
                      pl.BlockSpec((1, tc, E), lambda b, i, ps: (b, i, 0)),
                      pl.BlockSpec((1,) + y.shape[1:], lambda b, i, ps: (b, 0, 0, 0), pipeline_mode=pl.Buffered(1)),
                      pl.BlockSpec(ln_final.shape, lambda b, i, ps: (0, 0))],
            out_specs=pl.BlockSpec((1, tc, D), lambda b, i, ps: (b, i, 0))),
        out_shape=jax.ShapeDtypeStruct((B, S, D), F32),
        compiler_params=_cparams(2),
        name="combine",
    )(pstart, x2, slot_t, y, ln_final)


def _stacked_weights(p):
    w_in = p["w_in"]
    o1 = GQA_HEADS * GQA_HEAD_DIM
    o2 = o1 + GQA_KV_HEADS * GQA_HEAD_DIM
    o3 = o2 + GQA_KV_HEADS * GQA_HEAD_DIM
    o4 = o3 + MLA_Q_RANK
    o5 = o4 + MLA_KV_RANK
    heads = lambda w, n: w.reshape(w.shape[:-1] + (n, w.shape[-1] // n))
    flat = lambda w: w.reshape(w.shape[:-2] + (-1,))
    wq = flat(_lay_gqa(heads(w_in[..., :o1], GQA_HEADS)))
    wk = flat(_lay_gqa(heads(w_in[..., o1:o2], GQA_KV_HEADS)))
    wv = flat(_lay_v(heads(w_in[..., o2:o3], GQA_KV_HEADS)))
    w_kr = w_in[..., o5:]
    wkr = _lay_mla(jnp.zeros(w_kr.shape[:-1] + (MLA_NOPE_DIM,), F32), w_kr)
    wcat = jnp.concatenate([wq, wk, wv, w_in[..., o3:o4], w_in[..., o4:o5], wkr], axis=-1).astype(BF16)

    wqb = heads(p["w_q_b"], MLA_HEADS)
    wqb = flat(_lay_mla(wqb[..., :MLA_NOPE_DIM], wqb[..., MLA_NOPE_DIM:])).astype(BF16)
    wkvb = heads(p["w_kv_b"], MLA_HEADS)
    k_nope = wkvb[..., :MLA_NOPE_DIM]
    wkb = flat(_lay_mla(k_nope, jnp.zeros(k_nope.shape[:-1] + (MLA_ROPE_DIM,), F32))).astype(BF16)
    wvb = flat(_lay_v(wkvb[..., MLA_NOPE_DIM:])).astype(BF16)

    q_scale = GQA_HEAD_DIM ** -0.5 * LOG2E
    row = lambda v: v[:, None, :]
    return {
        "ln_mix": row(p["ln_mix"]), "wcat": wcat, "wqb": wqb, "wkb": wkb, "wvb": wvb,
        "gq": row(_lay_gqa(p["gqa_q_norm"] * q_scale)), "gk": row(_lay_gqa(p["gqa_k_norm"])),
        "gql": row(p["mla_q_norm"]), "gkv": row(p["mla_kv_norm"]),
        "on": row(jnp.concatenate([p["out_norm_gqa"], p["out_norm_mla"]], axis=-1)),
        "w_o": p["w_o"].astype(BF16), "ln_mem": row(p["ln_mem"]), "ln_mem_kv": row(p["ln_mem_kv"]),
        "w_mem_q": p["w_mem_q"].astype(BF16), "w_mem_kv": p["w_mem_kv"].astype(BF16),
        "w_mem_o": p["w_mem_o"].astype(BF16), "ln_ffn": row(p["ln_ffn"]),
        "w_router_t": _split_hi_lo(jnp.swapaxes(p["w_router"], 1, 2)),
    }


def _split_hi_lo(w):
    hi = w.astype(BF16)
    lo = (w - hi.astype(F32)).astype(BF16)
    return jnp.concatenate([hi, lo], axis=1)


def _tables(seq_len):
    cos_g, sin_g, cos_m, sin_m = _rope_tables(seq_len)
    mq_scale = MLA_QK_DIM ** -0.5 * LOG2E
    one_g = jnp.zeros((GQA_KV_HEADS, LANES), F32).at[:, ONE_LANE].set(1.0)
    one_m = jnp.zeros((MLA_HEADS, LANES), F32).at[:, ONE_LANE].set(1.0)
    return {"cos_g": cos_g, "sin_g": sin_g, "cos_m": cos_m, "sin_m": sin_m,
            "cos_mq": cos_m * mq_scale, "sin_mq": sin_m * mq_scale,
            "one_g": one_g.reshape(1, -1), "one_m": one_m.reshape(1, -1)}


def _pick(n, pref):
    t = min(n, pref)
    assert n % t == 0, (n, t)
    return t


def kernel(x, mem, ln_mix, w_in, gqa_q_norm, gqa_k_norm, mla_q_norm, mla_kv_norm, w_q_b, w_kv_b, out_norm_gqa,
           out_norm_mla, w_o, ln_mem, ln_mem_kv, w_mem_q, w_mem_kv, w_mem_o, ln_ffn, w_router, w_gate, w_up,
           w_down, ln_final):
    p = dict(ln_mix=ln_mix, w_in=w_in, gqa_q_norm=gqa_q_norm, gqa_k_norm=gqa_k_norm, mla_q_norm=mla_q_norm,
             mla_kv_norm=mla_kv_norm, w_q_b=w_q_b, w_kv_b=w_kv_b, out_norm_gqa=out_norm_gqa,
             out_norm_mla=out_norm_mla, w_o=w_o, ln_mem=ln_mem, ln_mem_kv=ln_mem_kv, w_mem_q=w_mem_q,
             w_mem_kv=w_mem_kv, w_mem_o=w_mem_o, ln_ffn=ln_ffn, w_router=w_router, w_gate=w_gate, w_up=w_up,
             w_down=w_down)
    B, S, D = x.shape
    depth = w_in.shape[0]
    E = N_EXPERTS
    assert S % LANES == 0 and S % GRID_W == 0
    cap = EC_CAPACITY_FACTOR * S // E
    chunks = S // LANES
    tabs = _tables(S)
    ln_final2 = ln_final.reshape(1, -1)
    ts_in, ts_post, tq, tk, tc = _pick(S, 256), _pick(S, 1024), _pick(S, 2048), _pick(S, 512), _pick(S, 256)
    sw = _stacked_weights(p)

    for l in range(depth):
        q_all, k_all, v_all = _mixer_in(x, sw, l, tabs, ts_in)
        o = _attention(q_all, k_all, v_all, tq, tk)
        kv_mem = _mem_kv(mem, sw, l)
        x2, h3, aff = _post_attn(o, x, kv_mem, sw, l, ts_post)
        slot, pos = _select(aff.reshape(B, E * chunks, LANES), cap, chunks)
        slot = slot.reshape(B, E, S)
        pstart = pos[:, :, 0].reshape(B * E * chunks)
        y = _experts(pstart, slot.reshape(B, E, 1, S), aff.reshape(B, E, 1, S), h3, w_gate, w_up, w_down, l,
                     cap, chunks)
        x = _combine(pstart, x2, jnp.swapaxes(slot, 1, 2), y, ln_final2, chunks, tc, final=(l == depth - 1))
    return x
```

```python
import functools
import math

import numpy as np
import jax
import jax.numpy as jnp
from jax import lax
from jax.experimental import pallas as pl
from jax.experimental.pallas import tpu as pltpu

F32 = jnp.float32
BF16 = jnp.bfloat16

GRID_W = 64
ROPE_THETA = 10000.0
EPS = 1e-6
GQA_HEADS = 8
GQA_KV_HEADS = 2
GQA_GROUP = GQA_HEADS // GQA_KV_HEADS
GQA_HEAD_DIM = 64
MLA_HEADS = 8
MLA_Q_RANK = 256
MLA_KV_RANK = 128
MLA_NOPE_DIM = 64
MLA_ROPE_DIM = 32
MLA_V_DIM = 64
MLA_QK_DIM = MLA_NOPE_DIM + MLA_ROPE_DIM
MEM_HEADS = 4
MEM_HEAD_DIM = 128
N_EXPERTS = 16
EC_CAPACITY_FACTOR = 2

LANES = 128
LOG2E = math.log2(math.e)
VMEM_LIMIT = 56 * 1024 * 1024

N_HEADS = GQA_HEADS + MLA_HEADS
N_PAIRS = N_HEADS // 2
GQA_PAIRS = GQA_HEADS // 2


def _cparams(n_axes):
    return pltpu.CompilerParams(dimension_semantics=("arbitrary",) * n_axes, vmem_limit_bytes=VMEM_LIMIT)


def _rms(x, eps=EPS):
    return x * lax.rsqrt(jnp.mean(x * x, axis=-1, keepdims=True) + eps)


def _dot(a, b):
    return jnp.dot(a, b, preferred_element_type=F32)


def _dot_nt(a, b):
    return lax.dot_general(a, b, (((1,), (1,)), ((), ())), preferred_element_type=F32)


V_DIM = 64
ONE_LANE = V_DIM


def _zeros_like_cols(w, n):
    return jnp.zeros(w.shape[:-1] + (n,), w.dtype)


def _lay_gqa(w):
    z = _zeros_like_cols(w, 32)
    return jnp.concatenate([w[..., :32], z, w[..., 32:], z], axis=-1)


def _lay_mla(nope, rope):
    z = _zeros_like_cols(nope, 16)
    return jnp.concatenate([nope[..., :32], rope[..., :16], z, nope[..., 32:], rope[..., 16:], z], axis=-1)


def _lay_v(v):
    return jnp.concatenate([v, _zeros_like_cols(v, LANES - V_DIM)], axis=-1)


def _rope_tables(seq_len):
    rows = seq_len // GRID_W
    row = jnp.repeat(jnp.arange(rows, dtype=F32), GRID_W)
    col = jnp.tile(jnp.arange(GRID_W, dtype=F32), rows)

    def angles(rot_dim):
        axis_dim = rot_dim // 2
        inv_freq = ROPE_THETA ** (-jnp.arange(0, axis_dim, 2, dtype=F32) / axis_dim)
        ang = jnp.concatenate([row[:, None] * inv_freq[None, :], col[:, None] * inv_freq[None, :]], axis=-1)
        return jnp.cos(ang), jnp.sin(ang)

    cg, sg = angles(GQA_HEAD_DIM)
    cm, sm = angles(MLA_ROPE_DIM)
    cos_g = _lay_gqa(jnp.concatenate([cg, cg], axis=-1))
    sin_g = _lay_gqa(jnp.concatenate([-sg, sg], axis=-1))
    one = jnp.ones((seq_len, MLA_NOPE_DIM), F32)
    cos_m = _lay_mla(one, jnp.concatenate([cm, cm], axis=-1))
    sin_m = _lay_mla(0.0 * one, jnp.concatenate([-sm, sm], axis=-1))
    return cos_g, sin_g, cos_m, sin_m


_W_SPLITS = (GQA_HEADS * LANES, GQA_KV_HEADS * LANES, GQA_KV_HEADS * LANES, MLA_Q_RANK, MLA_KV_RANK, LANES)
_W_OFFS = tuple(int(v) for v in np.cumsum((0,) + _W_SPLITS))


def _mixer_in_kernel(x_ref, ln_ref, wcat_ref, wqb_ref, wkb_ref, wvb_ref, gq_ref, gk_ref, gql_ref, gkv_ref,
                     cg_ref, sg_ref, cmq_ref, smq_ref, cmk_ref, smk_ref, oneg_ref, onem_ref,
                     q_ref, k_ref, v_ref):
    o = _W_OFFS
    n_vg = GQA_KV_HEADS * LANES
    h = (_rms(x_ref[0]) * ln_ref[0]).astype(BF16)
    proj = _dot(h, wcat_ref[0])
    cg, sg = cg_ref[...], sg_ref[...]

    def head_norm_rope(blk, gain):
        ss = jnp.sum(blk * blk, axis=-1, keepdims=True) * (1.0 / GQA_HEAD_DIM)
        y = blk * lax.rsqrt(ss + EPS) * gain
        return y * cg + pltpu.roll(y, 64, 1) * sg

    for j in range(GQA_HEADS):
        blk = proj[:, o[0] + j * LANES:o[0] + (j + 1) * LANES]
        q_ref[0, :, j * LANES:(j + 1) * LANES] = head_norm_rope(blk, gq_ref[0]).astype(BF16)
    for j in range(GQA_KV_HEADS):
        blk = proj[:, o[1] + j * LANES:o[1] + (j + 1) * LANES]
        k_ref[0, :, j * LANES:(j + 1) * LANES] = head_norm_rope(blk, gk_ref[0]).astype(BF16)
    v_ref[0, :, 0:n_vg] = (proj[:, o[2]:o[3]] + oneg_ref[...]).astype(BF16)

    c_q = (_rms(proj[:, o[3]:o[4]]) * gql_ref[0]).astype(BF16)
    qm = _dot(c_q, wqb_ref[0])
    cmq, smq = cmq_ref[...], smq_ref[...]
    for j in range(MLA_HEADS):
        blk = qm[:, j * LANES:(j + 1) * LANES]
        q_ref[0, :, (GQA_HEADS + j) * LANES:(GQA_HEADS + j + 1) * LANES] = (
            blk * cmq + pltpu.roll(blk, 64, 1) * smq).astype(BF16)

    c_kv = (_rms(proj[:, o[4]:o[5]]) * gkv_ref[0]).astype(BF16)
    kn = _dot(c_kv, wkb_ref[0])
    vm = _dot(c_kv, wvb_ref[0]) + onem_ref[...]
    kr = proj[:, o[5]:o[6]]
    kr = kr * cmk_ref[...] + pltpu.roll(kr, 64, 1) * smk_ref[...]
    for j in range(MLA_HEADS):
        k_ref[0, :, (GQA_KV_HEADS + j) * LANES:(GQA_KV_HEADS + j + 1) * LANES] = (
            kn[:, j * LANES:(j + 1) * LANES] + kr).astype(BF16)
    v_ref[0, :, n_vg:] = vm.astype(BF16)


def _layer_spec(a, l):
    zeros = (0,) * (a.ndim - 1)
    return pl.BlockSpec((1,) + a.shape[1:], lambda *grid_ids: (l,) + zeros)


def _mixer_in(x, sw, l, tabs, ts):
    B, S, D = x.shape
    nq, nk = N_HEADS * LANES, (GQA_KV_HEADS + MLA_HEADS) * LANES
    nv = nk
    full = lambda a: pl.BlockSpec(a.shape, lambda b, i: (0,) * a.ndim)
    tab = pl.BlockSpec((ts, LANES), lambda b, i: (i, 0))
    consts = (sw["ln_mix"], sw["wcat"], sw["wqb"], sw["wkb"], sw["wvb"], sw["gq"], sw["gk"], sw["gql"], sw["gkv"])
    return pl.pallas_call(
        _mixer_in_kernel,
        grid=(B, S // ts),
        in_specs=[pl.BlockSpec((1, ts, D), lambda b, i: (b, i, 0))] + [_layer_spec(a, l) for a in consts]
        + [tab] * 6 + [full(tabs["one_g"]), full(tabs["one_m"])],
        out_specs=[pl.BlockSpec((1, ts, nq), lambda b, i: (b, i, 0)),
                   pl.BlockSpec((1, ts, nk), lambda b, i: (b, i, 0)),
                   pl.BlockSpec((1, ts, nv), lambda b, i: (b, i, 0))],
        out_shape=[jax.ShapeDtypeStruct((B, S, nq), BF16), jax.ShapeDtypeStruct((B, S, nk), BF16),
                   jax.ShapeDtypeStruct((B, S, nv), BF16)],
        compiler_params=_cparams(2),
        name="mixer_in",
    )(x, *consts, tabs["cos_g"], tabs["sin_g"], tabs["cos_mq"], tabs["sin_mq"], tabs["cos_m"], tabs["sin_m"],
      tabs["one_g"], tabs["one_m"])


SCORE_BOUND_MAX = 40.0
BOUND_SLACK = 1.02


def _dot_tn(a, b):
    return lax.dot_general(a, b, (((0,), (0,)), ((), ())), preferred_element_type=F32)


def _attn_kernel(q_ref, ka_ref, kb_ref, va_ref, vb_ref, o_ref, kmax_ref, *, tk):
    tq = q_ref.shape[1]
    n_chunks = ka_ref.shape[1] // tk
    ones8 = jnp.ones((8, LANES), BF16)
    k_refs = (ka_ref, kb_ref)
    v_refs = (va_ref, vb_ref)

    @pl.when(pl.program_id(2) == 0)
    def _():
        for h in range(2):
            kk = k_refs[h][0]
            ksq = _dot_nt(ones8, kk * kk)
            kmax_ref[h] = jnp.broadcast_to(jnp.max(ksq, axis=-1, keepdims=True), (8, LANES))

    qs = (q_ref[0, :, 0:LANES], q_ref[0, :, LANES:2 * LANES])
    bounds = []
    for h in range(2):
        qsq = _dot_nt(ones8, qs[h] * qs[h])[0:1]
        bounds.append(jnp.sqrt(qsq * kmax_ref[h][0:1, 0:1]) * BOUND_SLACK)
    bound_max = jnp.max(jnp.maximum(bounds[0], bounds[1]))

    def finish(accs):
        outs = [a[0:V_DIM] * (1.0 / a[ONE_LANE:ONE_LANE + 1]) for a in accs]
        o_ref[0] = jnp.concatenate(outs, axis=0).T

    def bounded():
        accs = [jnp.zeros((LANES, tq), F32), jnp.zeros((LANES, tq), F32)]
        for c in range(n_chunks):
            for h in range(2):
                ks = k_refs[h][0, c * tk:(c + 1) * tk, :]
                vs = v_refs[h][0, c * tk:(c + 1) * tk, :]
                pt = jnp.exp2(_dot_nt(ks, qs[h]) - bounds[h]).astype(BF16)
                accs[h] = accs[h] + _dot_tn(vs, pt)
        finish(accs)

    def running_max():
        accs = []
        for h in range(2):
            def body(c, carry):
                m, acc = carry
                start = pl.multiple_of(c * tk, tk)
                ks = k_refs[h][0, pl.ds(start, tk), :]
                vs = v_refs[h][0, pl.ds(start, tk), :]
                st = _dot_nt(ks, qs[h])
                m_new = jnp.maximum(m, jnp.max(st, axis=0, keepdims=True))
                pt = jnp.exp2(st - m_new).astype(BF16)
                return m_new, jnp.exp2(m - m_new) * acc + _dot_tn(vs, pt)

            init = (jnp.full((1, tq), -jnp.inf, F32), jnp.zeros((LANES, tq), F32))
            accs.append(lax.fori_loop(0, n_chunks, body, init)[1])
        finish(accs)

    pl.when(bound_max <= SCORE_BOUND_MAX)(bounded)
    pl.when(jnp.logical_not(bound_max <= SCORE_BOUND_MAX))(running_max)


def _attention(q_all, k_all, v_all, tq, tk):
    B, S, _ = q_all.shape
    pairs_per_kv = GQA_GROUP // 2

    def kv_col(p, second):
        return jnp.where(p < GQA_PAIRS, p // pairs_per_kv, GQA_KV_HEADS + 2 * (p - GQA_PAIRS) + second)

    kv_spec = lambda second: pl.BlockSpec((1, S, LANES), lambda b, p, i: (b, 0, kv_col(p, second)))
    return pl.pallas_call(
        functools.partial(_attn_kernel, tk=tk),
        grid=(B, N_PAIRS, S // tq),
        in_specs=[pl.BlockSpec((1, tq, 2 * LANES), lambda b, p, i: (b, i, p)),
                  kv_spec(0), kv_spec(1), kv_spec(0), kv_spec(1)],
        out_specs=pl.BlockSpec((1, tq, LANES), lambda b, p, i: (b, i, p)),
        out_shape=jax.ShapeDtypeStruct((B, S, N_PAIRS * LANES), F32),
        scratch_shapes=[pltpu.VMEM((2, 8, LANES), F32)],
        compiler_params=_cparams(3),
        name="attention",
    )(q_all, k_all, k_all, v_all, v_all)


def _mem_kv_kernel(m_ref, ln_ref, w_ref, o_ref):
    h = (_rms(m_ref[0]) * ln_ref[0]).astype(BF16)
    o_ref[0] = _dot(h, w_ref[0]).astype(BF16)


def _mem_kv(mem, sw, l):
    B, M, D = mem.shape
    ln, w = sw["ln_mem_kv"], sw["w_mem_kv"]
    n = w.shape[2]
    return pl.pallas_call(
        _mem_kv_kernel,
        grid=(B,),
        in_specs=[pl.BlockSpec((1, M, D), lambda b: (b, 0, 0)), _layer_spec(ln, l), _layer_spec(w, l)],
        out_specs=pl.BlockSpec((1, M, n), lambda b: (b, 0, 0)),
        out_shape=jax.ShapeDtypeStruct((B, M, n), BF16),
        compiler_params=_cparams(1),
        name="mem_kv",
    )(mem, ln, w)


def _post_attn_kernel(o_ref, x_ref, on_ref, wo_ref, lnm_ref, wmq_ref, kv_ref, wmo_ref, lnf_ref, wr_ref,
                      x2_ref, h3_ref, aff_ref):
    o = o_ref[0]
    half = o.shape[1] // 2
    merged = (jnp.concatenate([_rms(o[:, :half]), _rms(o[:, half:])], axis=-1) * on_ref[0]).astype(BF16)
    x1 = x_ref[0] + _dot(merged, wo_ref[0])

    h2 = (_rms(x1) * lnm_ref[0]).astype(BF16)
    q = (_dot(h2, wmq_ref[0]) * (MEM_HEAD_DIM ** -0.5 * LOG2E)).astype(BF16)
    kv = kv_ref[0]
    n_mem = MEM_HEADS * MEM_HEAD_DIM
    outs = []
    for hh in range(MEM_HEADS):
        lo, hi = hh * MEM_HEAD_DIM, (hh + 1) * MEM_HEAD_DIM
        s = _dot_nt(q[:, lo:hi], kv[:, lo:hi])
        p = jnp.exp2(s - jnp.max(s, axis=-1, keepdims=True))
        l = jnp.sum(p, axis=-1, keepdims=True)
        outs.append(_dot(p.astype(BF16), kv[:, n_mem + lo:n_mem + hi]) * (1.0 / l))
    oc = jnp.concatenate(outs, axis=-1).astype(BF16)
    x2 = x1 + _dot(oc, wmo_ref[0])
    x2_ref[0] = x2

    h3 = _rms(x2) * lnf_ref[0]
    h3_ref[0] = h3.astype(BF16)
    h_hi = h3.astype(BF16)
    h_lo = (h3 - h_hi.astype(F32)).astype(BF16)
    n_e = wr_ref.shape[1] // 2
    part = _dot_nt(wr_ref[0], h_hi)
    logits = part[:n_e] + part[n_e:] + _dot_nt(wr_ref[0, 0:n_e, :], h_lo)
    e = jnp.exp(logits - jnp.max(logits, axis=0, keepdims=True))
    aff_ref[0] = e * (1.0 / jnp.sum(e, axis=0, keepdims=True))


def _post_attn(o, x, kv_mem, sw, l, ts):
    B, S, D = x.shape
    E = N_EXPERTS
    consts_a = (sw["on"], sw["w_o"], sw["ln_mem"], sw["w_mem_q"])
    consts_b = (sw["w_mem_o"], sw["ln_ffn"], sw["w_router_t"])
    tile = lambda n: pl.BlockSpec((1, ts, n), lambda b, i: (b, i, 0))
    return pl.pallas_call(
        _post_attn_kernel,
        grid=(B, S // ts),
        in_specs=[tile(o.shape[2]), tile(D)] + [_layer_spec(a, l) for a in consts_a]
        + [pl.BlockSpec((1,) + kv_mem.shape[1:], lambda b, i: (b, 0, 0))] + [_layer_spec(a, l) for a in consts_b],
        out_specs=[tile(D), tile(D), pl.BlockSpec((1, E, ts), lambda b, i: (b, 0, i))],
        out_shape=[jax.ShapeDtypeStruct((B, S, D), F32), jax.ShapeDtypeStruct((B, S, D), BF16),
                   jax.ShapeDtypeStruct((B, E, S), F32)],
        compiler_params=_cparams(2),
        name="post_attn",
    )(o, x, *consts_a, kv_mem, *consts_b)


def _select_kernel(aff_ref, slot_ref, pos_ref, *, cap, chunks, seq_rows):
    a = aff_ref[...]
    n_rows = a.shape[0]
    bits = pltpu.bitcast(a, jnp.int32)

    r_i = lax.broadcasted_iota(jnp.int32, (seq_rows, seq_rows), 0)
    c_i = lax.broadcasted_iota(jnp.int32, (seq_rows, seq_rows), 1)
    earlier_chunk = ((r_i // chunks) == (c_i // chunks)) & (c_i < r_i)
    bd_before = jnp.where(earlier_chunk, 1.0, 0.0).astype(BF16)
    l_r = lax.broadcasted_iota(jnp.int32, (LANES, LANES), 0)
    l_c = lax.broadcasted_iota(jnp.int32, (LANES, LANES), 1)
    ones = jnp.ones((LANES, LANES), BF16)
    before = jnp.where(l_r < l_c, 1.0, 0.0).astype(BF16)

    def as01(mask):
        return jnp.where(mask, 1.0, 0.0).astype(BF16)

    def expert_count(x01):
        per_chunk = _dot(x01, ones).reshape(n_rows // chunks, chunks, LANES)
        total = jnp.sum(per_chunk, axis=1, keepdims=True)
        return jnp.broadcast_to(total, per_chunk.shape).reshape(n_rows, LANES)

    def prefix_excl(x01):
        per_chunk = _dot(x01, ones).astype(BF16)
        earlier = [_dot(bd_before, per_chunk[s * seq_rows:(s + 1) * seq_rows]) for s in range(n_rows // seq_rows)]
        return _dot(x01, before) + jnp.concatenate(earlier, axis=0)

    def step(i, theta):
        cand = theta | (jnp.int32(1) << (30 - i))
        cnt = expert_count(as01(bits >= cand))
        return jnp.where(cnt >= cap, cand, theta)

    theta = lax.fori_loop(0, 31, step, jnp.zeros(bits.shape, jnp.int32))
    gt = bits > theta
    eq = bits == theta
    need = cap - expert_count(as01(gt))
    sel = gt | (eq & (prefix_excl(as01(eq)) < need))
    pos = prefix_excl(as01(sel))
    slot_ref[...] = jnp.where(sel, pos, -1.0).astype(jnp.int32)
    pos_ref[...] = pos.astype(jnp.int32)


def _select(aff2, cap, chunks):
    B, seq_rows, _ = aff2.shape
    assert chunks % 8 == 0, "the per-expert reduction reshapes rows into whole sublane tiles"
    n_rows = B * seq_rows
    blk = pl.BlockSpec((n_rows, LANES), lambda i: (0, 0))
    slot, pos = pl.pallas_call(
        functools.partial(_select_kernel, cap=cap, chunks=chunks, seq_rows=seq_rows),
        grid=(1,),
        in_specs=[blk],
        out_specs=[blk, blk],
        out_shape=[jax.ShapeDtypeStruct((n_rows, LANES), jnp.int32)] * 2,
        compiler_params=_cparams(1),
        name="select",
    )(aff2.reshape(n_rows, LANES))
    return slot.reshape(aff2.shape), pos.reshape(aff2.shape)


GATHER_TILE = 2 * LANES
GATHER_WIN = GATHER_TILE + 8
SCATTER_TILE = LANES
SCATTER_WIN = 2 * LANES


def _y_rows(cap):
    return -(-cap // LANES) * LANES + LANES


def _expert_kernel(ps_ref, slot_ref, aff_ref, h_ref, wg_ref, wu_ref, wd_ref, y_ref, xin_ref, gate_ref, *, cap, chunks):
    base = (pl.program_id(0) * N_EXPERTS + pl.program_id(1)) * chunks
    S = h_ref.shape[1]
    xin_ref[...] = jnp.zeros_like(xin_ref)
    gate_ref[...] = jnp.zeros_like(gate_ref)
    row = lax.broadcasted_iota(jnp.int32, (GATHER_WIN, GATHER_TILE), 0)
    for t in range(S // GATHER_TILE):
        lo, hi = t * GATHER_TILE, (t + 1) * GATHER_TILE
        start = pl.multiple_of((ps_ref[base + t * (GATHER_TILE // LANES)] >> 3) << 3, 8)
        hit = (row + start) == slot_ref[0, 0, :, lo:hi]
        onehot = jnp.where(hit, 1.0, 0.0).astype(BF16)
        xin_ref[pl.ds(start, GATHER_WIN), :] += _dot(onehot, h_ref[0, lo:hi, :])
        g = jnp.sum(jnp.where(hit, aff_ref[0, 0, :, lo:hi], 0.0), axis=-1, keepdims=True)
        gate_ref[pl.ds(start, GATHER_WIN), :] += jnp.broadcast_to(g, (GATHER_WIN, LANES))
    x_in = xin_ref[0:cap, :].astype(BF16)
    a = _dot(x_in, wg_ref[0, 0].astype(BF16))
    u = _dot(x_in, wu_ref[0, 0].astype(BF16))
    hm = (a * (1.0 / (1.0 + jnp.exp(-a))) * u).astype(BF16)
    y_ref[0, 0, 0:cap, :] = (_dot(hm, wd_ref[0, 0].astype(BF16)) * gate_ref[0:cap, 0:1]).astype(BF16)
    y_ref[0, 0, cap:, :] = jnp.zeros((y_ref.shape[2] - cap, y_ref.shape[3]), BF16)


def _experts(pstart, slot_row, aff_row, h3, wg, wu, wd, l, cap, chunks):
    B, S, D = h3.shape
    E = N_EXPERTS
    F = wg.shape[3]
    assert S % GATHER_TILE == 0 and cap % 8 == 0
    row = pl.BlockSpec((1, 1, 1, S), lambda b, e, ps: (b, e, 0, 0))
    return pl.pallas_call(
        functools.partial(_expert_kernel, cap=cap, chunks=chunks),
        grid_spec=pltpu.PrefetchScalarGridSpec(
            num_scalar_prefetch=1,
            grid=(B, E),
            in_specs=[row, row, pl.BlockSpec((1, S, D), lambda b, e, ps: (b, 0, 0)),
                      pl.BlockSpec((1, 1, D, F), lambda b, e, ps: (l, e, 0, 0)),
                      pl.BlockSpec((1, 1, D, F), lambda b, e, ps: (l, e, 0, 0)),
                      pl.BlockSpec((1, 1, F, D), lambda b, e, ps: (l, e, 0, 0))],
            out_specs=pl.BlockSpec((1, 1, _y_rows(cap), D), lambda b, e, ps: (b, e, 0, 0)),
            scratch_shapes=[pltpu.VMEM((cap + GATHER_WIN, D), F32), pltpu.VMEM((cap + GATHER_WIN, LANES), F32)]),
        out_shape=jax.ShapeDtypeStruct((B, E, _y_rows(cap), D), BF16),
        compiler_params=_cparams(2),
        name="experts",
    )(pstart, slot_row, aff_row, h3, wg, wu, wd)


def _combine_kernel(ps_ref, x_ref, slot_ref, y_ref, lnf_ref, o_ref, *, chunks, final):
    tc = x_ref.shape[1]
    b, i = pl.program_id(0), pl.program_id(1)
    col = lax.broadcasted_iota(jnp.int32, (SCATTER_TILE, SCATTER_WIN), 1)
    for sub in range(tc // SCATTER_TILE):
        lo, hi = sub * SCATTER_TILE, (sub + 1) * SCATTER_TILE
        chunk = i * (tc // SCATTER_TILE) + sub
        slot_t = slot_ref[0, lo:hi, :]
        hots, wins = [], []
        for e in range(N_EXPERTS):
            first = (ps_ref[(b * N_EXPERTS + e) * chunks + chunk] >> 7) << 7
            start = pl.multiple_of(jnp.minimum(first, y_ref.shape[2] - SCATTER_WIN), LANES)
            hots.append(jnp.where(slot_t[:, e:e + 1] == col + start, 1.0, 0.0).astype(BF16))
            wins.append(y_ref[0, e, pl.ds(start, SCATTER_WIN), :])
        acc = x_ref[0, lo:hi, :] + _dot(jnp.concatenate(hots, axis=1), jnp.concatenate(wins, axis=0))
        if final:
            acc = _rms(acc) * lnf_ref[...]
        o_ref[0, lo:hi, :] = acc


def _combine(pstart, x2, slot_t, y, ln_final, chunks, tc, final):
    B, S, D = x2.shape
    E = N_EXPERTS
    assert tc % SCATTER_TILE == 0 and SCATTER_TILE == LANES
    return pl.pallas_call(
        functools.partial(_combine_kernel, chunks=chunks, final=final),
        grid_spec=pltpu.PrefetchScalarGridSpec(
            num_scalar_prefetch=1,
            grid=(B, S // tc),
            in_specs=[pl.BlockSpec((1, tc, D), lambda b, i, ps: (b, i, 0)),
                      pl.BlockSpec((1, tc, E), lambda b, i, ps: (b, i, 0)),
                      pl.BlockSpec((1,) + y.shape[1:], lambda b, i, ps: (b, 0, 0, 0), pipeline_mode=pl.Buffered(1)),
                      pl.BlockSpec(ln_final.shape, lambda b, i, ps: (0, 0))],
            out_specs=pl.BlockSpec((1, tc, D), lambda b, i, ps: (b, i, 0))),
        out_shape=jax.ShapeDtypeStruct((B, S, D), F32),
        compiler_params=_cparams(2),
        name="combine",
    )(pstart, x2, slot_t, y, ln_final)


def _stacked_weights(p):
    w_in = p["w_in"]
    o1 = GQA_HEADS * GQA_HEAD_DIM
    o2 = o1 + GQA_KV_HEADS * GQA_HEAD_DIM
    o3 = o2 + GQA_KV_HEADS * GQA_HEAD_DIM
    o4 = o3 + MLA_Q_RANK
    o5 = o4 + MLA_KV_RANK
    heads = lambda w, n: w.reshape(w.shape[:-1] + (n, w.shape[-1] // n))
    flat = lambda w: w.reshape(w.shape[:-2] + (-1,))
    wq = flat(_lay_gqa(heads(w_in[..., :o1], GQA_HEADS)))
    wk = flat(_lay_gqa(heads(w_in[..., o1:o2], GQA_KV_HEADS)))
    wv = flat(_lay_v(heads(w_in[..., o2:o3], GQA_KV_HEADS)))
    w_kr = w_in[..., o5:]
    wkr = _lay_mla(jnp.zeros(w_kr.shape[:-1] + (MLA_NOPE_DIM,), F32), w_kr)
    wcat = jnp.concatenate([wq, wk, wv, w_in[..., o3:o4], w_in[..., o4:o5], wkr], axis=-1).astype(BF16)

    wqb = heads(p["w_q_b"], MLA_HEADS)
    wqb = flat(_lay_mla(wqb[..., :MLA_NOPE_DIM], wqb[..., MLA_NOPE_DIM:])).astype(BF16)
    wkvb = heads(p["w_kv_b"], MLA_HEADS)
    k_nope = wkvb[..., :MLA_NOPE_DIM]
    wkb = flat(_lay_mla(k_nope, jnp.zeros(k_nope.shape[:-1] + (MLA_ROPE_DIM,), F32))).astype(BF16)
    wvb = flat(_lay_v(wkvb[..., MLA_NOPE_DIM:])).astype(BF16)

    q_scale = GQA_HEAD_DIM ** -0.5 * LOG2E
    row = lambda v: v[:, None, :]
    return {
        "ln_mix": row(p["ln_mix"]), "wcat": wcat, "wqb": wqb, "wkb": wkb, "wvb": wvb,
        "gq": row(_lay_gqa(p["gqa_q_norm"] * q_scale)), "gk": row(_lay_gqa(p["gqa_k_norm"])),
        "gql": row(p["mla_q_norm"]), "gkv": row(p["mla_kv_norm"]),
        "on": row(jnp.concatenate([p["out_norm_gqa"], p["out_norm_mla"]], axis=-1)),
        "w_o": p["w_o"].astype(BF16), "ln_mem": row(p["ln_mem"]), "ln_mem_kv": row(p["ln_mem_kv"]),
        "w_mem_q": p["w_mem_q"].astype(BF16), "w_mem_kv": p["w_mem_kv"].astype(BF16),
        "w_mem_o": p["w_mem_o"].astype(BF16), "ln_ffn": row(p["ln_ffn"]),
        "w_router_t": _split_hi_lo(jnp.swapaxes(p["w_router"], 1, 2)),
    }


def _split_hi_lo(w):
    hi = w.astype(BF16)
    lo = (w - hi.astype(F32)).astype(BF16)
    return jnp.concatenate([hi, lo], axis=1)


def _tables(seq_len):
    cos_g, sin_g, cos_m, sin_m = _rope_tables(seq_len)
    mq_scale = MLA_QK_DIM ** -0.5 * LOG2E
    one_g = jnp.zeros((GQA_KV_HEADS, LANES), F32).at[:, ONE_LANE].set(1.0)
    one_m = jnp.zeros((MLA_HEADS, LANES), F32).at[:, ONE_LANE].set(1.0)
    return {"cos_g": cos_g, "sin_g": sin_g, "cos_m": cos_m, "sin_m": sin_m,
            "cos_mq": cos_m * mq_scale, "sin_mq": sin_m * mq_scale,
            "one_g": one_g.reshape(1, -1), "one_m": one_m.reshape(1, -1)}


def _pick(n, pref):
    t = min(n, pref)
    assert n % t == 0, (n, t)
    return t


def kernel(x, mem, ln_mix, w_in, gqa_q_norm, gqa_k_norm, mla_q_norm, mla_kv_norm, w_q_b, w_kv_b, out_norm_gqa,
           out_norm_mla, w_o, ln_mem, ln_mem_kv, w_mem_q, w_mem_kv, w_mem_o, ln_ffn, w_router, w_gate, w_up,
           w_down, ln_final):
    p = dict(ln_mix=ln_mix, w_in=w_in, gqa_q_norm=gqa_q_norm, gqa_k_norm=gqa_k_norm, mla_q_norm=mla_q_norm,
             mla_kv_norm=mla_kv_norm, w_q_b=w_q_b, w_kv_b=w_kv_b, out_norm_gqa=out_norm_gqa,
             out_norm_mla=out_norm_mla, w_o=w_o, ln_mem=ln_mem, ln_mem_kv=ln_mem_kv, w_mem_q=w_mem_q,
             w_mem_kv=w_mem_kv, w_mem_o=w_mem_o, ln_ffn=ln_ffn, w_router=w_router, w_gate=w_gate, w_up=w_up,
             w_down=w_down)
    B, S, D = x.shape
    depth = w_in.shape[0]
    E = N_EXPERTS
    assert S % LANES == 0 and S % GRID_W == 0
    cap = EC_CAPACITY_FACTOR * S // E
    chunks = S // LANES
    tabs = _tables(S)
    ln_final2 = ln_final.reshape(1, -1)
    ts_in, ts_post, tq, tk, tc = _pick(S, 256), _pick(S, 1024), _pick(S, 4096), _pick(S, 512), _pick(S, 256)
    sw = _stacked_weights(p)

    for l in range(depth):
        q_all, k_all, v_all = _mixer_in(x, sw, l, tabs, ts_in)
        o = _attention(q_all, k_all, v_all, tq, tk)
        kv_mem = _mem_kv(mem, sw, l)
        x2, h3, aff = _post_attn(o, x, kv_mem, sw, l, ts_post)
        slot, pos = _select(aff.reshape(B, E * chunks, LANES), cap, chunks)
        slot = slot.reshape(B, E, S)
        pstart = pos[:, :, 0].reshape(B * E * chunks)
        y = _experts(pstart, slot.reshape(B, E, 1, S), aff.reshape(B, E, 1, S), h3, w_gate, w_up, w_down, l,
                     cap, chunks)
        x = _combine(pstart, x2, jnp.swapaxes(slot, 1, 2), y, ln_final2, chunks, tc, final=(l == depth - 1))
    return x
```

```python
import functools
import math

import numpy as np
import jax
import jax.numpy as jnp
from jax import lax
from jax.experimental import pallas as pl
from jax.experimental.pallas import tpu as pltpu

F32 = jnp.float32
BF16 = jnp.bfloat16

GRID_W = 64
ROPE_THETA = 10000.0
EPS = 1e-6
GQA_HEADS = 8
GQA_KV_HEADS = 2
GQA_GROUP = GQA_HEADS // GQA_KV_HEADS
GQA_HEAD_DIM = 64
MLA_HEADS = 8
MLA_Q_RANK = 256
MLA_KV_RANK = 128
MLA_NOPE_DIM = 64
MLA_ROPE_DIM = 32
MLA_V_DIM = 64
MLA_QK_DIM = MLA_NOPE_DIM + MLA_ROPE_DIM
MEM_HEADS = 4
MEM_HEAD_DIM = 128
N_EXPERTS = 16
EC_CAPACITY_FACTOR = 2

LANES = 128
LOG2E = math.log2(math.e)
VMEM_LIMIT = 56 * 1024 * 1024

N_HEADS = GQA_HEADS + MLA_HEADS
N_PAIRS = N_HEADS // 2
GQA_PAIRS = GQA_HEADS // 2


def _cparams(n_axes):
    return pltpu.CompilerParams(dimension_semantics=("arbitrary",) * n_axes, vmem_limit_bytes=VMEM_LIMIT)


def _rms(x, eps=EPS):
    return x * lax.rsqrt(jnp.mean(x * x, axis=-1, keepdims=True) + eps)


def _dot(a, b):
    return jnp.dot(a, b, preferred_element_type=F32)


def _dot_nt(a, b):
    return lax.dot_general(a, b, (((1,), (1,)), ((), ())), preferred_element_type=F32)


V_DIM = 64
ONE_LANE = V_DIM


def _zeros_like_cols(w, n):
    return jnp.zeros(w.shape[:-1] + (n,), w.dtype)


def _lay_gqa(w):
    z = _zeros_like_cols(w, 32)
    return jnp.concatenate([w[..., :32], z, w[..., 32:], z], axis=-1)


def _lay_mla(nope, rope):
    z = _zeros_like_cols(nope, 16)
    return jnp.concatenate([nope[..., :32], rope[..., :16], z, nope[..., 32:], rope[..., 16:], z], axis=-1)


def _lay_v(v):
    return jnp.concatenate([v, _zeros_like_cols(v, LANES - V_DIM)], axis=-1)


def _rope_tables(seq_len):
    rows = seq_len // GRID_W
    row = jnp.repeat(jnp.arange(rows, dtype=F32), GRID_W)
    col = jnp.tile(jnp.arange(GRID_W, dtype=F32), rows)

    def angles(rot_dim):
        axis_dim = rot_dim // 2
        inv_freq = ROPE_THETA ** (-jnp.arange(0, axis_dim, 2, dtype=F32) / axis_dim)
        ang = jnp.concatenate([row[:, None] * inv_freq[None, :], col[:, None] * inv_freq[None, :]], axis=-1)
        return jnp.cos(ang), jnp.sin(ang)

    cg, sg = angles(GQA_HEAD_DIM)
    cm, sm = angles(MLA_ROPE_DIM)
    cos_g = _lay_gqa(jnp.concatenate([cg, cg], axis=-1))
    sin_g = _lay_gqa(jnp.concatenate([-sg, sg], axis=-1))
    one = jnp.ones((seq_len, MLA_NOPE_DIM), F32)
    cos_m = _lay_mla(one, jnp.concatenate([cm, cm], axis=-1))
    sin_m = _lay_mla(0.0 * one, jnp.concatenate([-sm, sm], axis=-1))
    return cos_g, sin_g, cos_m, sin_m


_W_SPLITS = (GQA_HEADS * LANES, GQA_KV_HEADS * LANES, GQA_KV_HEADS * LANES, MLA_Q_RANK, MLA_KV_RANK, LANES)
_W_OFFS = tuple(int(v) for v in np.cumsum((0,) + _W_SPLITS))


def _mixer_in_kernel(x_ref, ln_ref, wcat_ref, wqb_ref, wkb_ref, wvb_ref, gq_ref, gk_ref, gql_ref, gkv_ref,
                     cg_ref, sg_ref, cmq_ref, smq_ref, cmk_ref, smk_ref, oneg_ref, onem_ref,
                     q_ref, k_ref, v_ref):
    o = _W_OFFS
    n_vg = GQA_KV_HEADS * LANES
    h = (_rms(x_ref[0]) * ln_ref[0]).astype(BF16)
    proj = _dot(h, wcat_ref[0])
    cg, sg = cg_ref[...], sg_ref[...]

    def head_norm_rope(blk, gain):
        ss = jnp.sum(blk * blk, axis=-1, keepdims=True) * (1.0 / GQA_HEAD_DIM)
        y = blk * lax.rsqrt(ss + EPS) * gain
        return y * cg + pltpu.roll(y, 64, 1) * sg

    for j in range(GQA_HEADS):
        blk = proj[:, o[0] + j * LANES:o[0] + (j + 1) * LANES]
        q_ref[0, :, j * LANES:(j + 1) * LANES] = head_norm_rope(blk, gq_ref[0]).astype(BF16)
    for j in range(GQA_KV_HEADS):
        blk = proj[:, o[1] + j * LANES:o[1] + (j + 1) * LANES]
        k_ref[0, :, j * LANES:(j + 1) * LANES] = head_norm_rope(blk, gk_ref[0]).astype(BF16)
    v_ref[0, :, 0:n_vg] = (proj[:, o[2]:o[3]] + oneg_ref[...]).astype(BF16)

    c_q = (_rms(proj[:, o[3]:o[4]]) * gql_ref[0]).astype(BF16)
    qm = _dot(c_q, wqb_ref[0])
    cmq, smq = cmq_ref[...], smq_ref[...]
    for j in range(MLA_HEADS):
        blk = qm[:, j * LANES:(j + 1) * LANES]
        q_ref[0, :, (GQA_HEADS + j) * LANES:(GQA_HEADS + j + 1) * LANES] = (
            blk * cmq + pltpu.roll(blk, 64, 1) * smq).astype(BF16)

    c_kv = (_rms(proj[:, o[4]:o[5]]) * gkv_ref[0]).astype(BF16)
    kn = _dot(c_kv, wkb_ref[0])
    vm = _dot(c_kv, wvb_ref[0]) + onem_ref[...]
    kr = proj[:, o[5]:o[6]]
    kr = kr * cmk_ref[...] + pltpu.roll(kr, 64, 1) * smk_ref[...]
    for j in range(MLA_HEADS):
        k_ref[0, :, (GQA_KV_HEADS + j) * LANES:(GQA_KV_HEADS + j + 1) * LANES] = (
            kn[:, j * LANES:(j + 1) * LANES] + kr).astype(BF16)
    v_ref[0, :, n_vg:] = vm.astype(BF16)


def _layer_spec(a, l):
    zeros = (0,) * (a.ndim - 1)
    return pl.BlockSpec((1,) + a.shape[1:], lambda *grid_ids: (l,) + zeros)


def _mixer_in(x, sw, l, tabs, ts):
    B, S, D = x.shape
    nq, nk = N_HEADS * LANES, (GQA_KV_HEADS + MLA_HEADS) * LANES
    nv = nk
    full = lambda a: pl.BlockSpec(a.shape, lambda b, i: (0,) * a.ndim)
    tab = pl.BlockSpec((ts, LANES), lambda b, i: (i, 0))
    consts = (sw["ln_mix"], sw["wcat"], sw["wqb"], sw["wkb"], sw["wvb"], sw["gq"], sw["gk"], sw["gql"], sw["gkv"])
    return pl.pallas_call(
        _mixer_in_kernel,
        grid=(B, S // ts),
        in_specs=[pl.BlockSpec((1, ts, D), lambda b, i: (b, i, 0))] + [_layer_spec(a, l) for a in consts]
        + [tab] * 6 + [full(tabs["one_g"]), full(tabs["one_m"])],
        out_specs=[pl.BlockSpec((1, ts, nq), lambda b, i: (b, i, 0)),
                   pl.BlockSpec((1, ts, nk), lambda b, i: (b, i, 0)),
                   pl.BlockSpec((1, ts, nv), lambda b, i: (b, i, 0))],
        out_shape=[jax.ShapeDtypeStruct((B, S, nq), BF16), jax.ShapeDtypeStruct((B, S, nk), BF16),
                   jax.ShapeDtypeStruct((B, S, nv), BF16)],
        compiler_params=_cparams(2),
        name="mixer_in",
    )(x, *consts, tabs["cos_g"], tabs["sin_g"], tabs["cos_mq"], tabs["sin_mq"], tabs["cos_m"], tabs["sin_m"],
      tabs["one_g"], tabs["one_m"])


SCORE_BOUND_MAX = 40.0
BOUND_SLACK = 1.02


def _dot_tn(a, b):
    return lax.dot_general(a, b, (((0,), (0,)), ((), ())), preferred_element_type=F32)


def _attn_kernel(q_ref, ka_ref, kb_ref, va_ref, vb_ref, o_ref, kmax_ref, *, tk):
    tq = q_ref.shape[1]
    n_chunks = ka_ref.shape[1] // tk
    ones8 = jnp.ones((8, LANES), BF16)
    k_refs = (ka_ref, kb_ref)
    v_refs = (va_ref, vb_ref)

    @pl.when(pl.program_id(2) == 0)
    def _():
        for h in range(2):
            kk = k_refs[h][0]
            ksq = _dot_nt(ones8, kk * kk)
            kmax_ref[h] = jnp.broadcast_to(jnp.max(ksq, axis=-1, keepdims=True), (8, LANES))

    qs = (q_ref[0, :, 0:LANES], q_ref[0, :, LANES:2 * LANES])
    bounds = []
    for h in range(2):
        qsq = _dot_nt(ones8, qs[h] * qs[h])[0:1]
        bounds.append(jnp.sqrt(qsq * kmax_ref[h][0:1, 0:1]) * BOUND_SLACK)
    bound_max = jnp.max(jnp.maximum(bounds[0], bounds[1]))

    def finish(accs):
        outs = [a[0:V_DIM] * (1.0 / a[ONE_LANE:ONE_LANE + 1]) for a in accs]
        o_ref[0] = jnp.concatenate(outs, axis=0).T

    def bounded():
        accs = [jnp.zeros((LANES, tq), F32), jnp.zeros((LANES, tq), F32)]
        for c in range(n_chunks):
            for h in range(2):
                ks = k_refs[h][0, c * tk:(c + 1) * tk, :]
                vs = v_refs[h][0, c * tk:(c + 1) * tk, :]
                pt = jnp.exp2(_dot_nt(ks, qs[h]) - bounds[h]).astype(BF16)
                accs[h] = accs[h] + _dot_tn(vs, pt)
        finish(accs)

    def running_max():
        accs = []
        for h in range(2):
            def body(c, carry):
                m, acc = carry
                start = pl.multiple_of(c * tk, tk)
                ks = k_refs[h][0, pl.ds(start, tk), :]
                vs = v_refs[h][0, pl.ds(start, tk), :]
                st = _dot_nt(ks, qs[h])
                m_new = jnp.maximum(m, jnp.max(st, axis=0, keepdims=True))
                pt = jnp.exp2(st - m_new).astype(BF16)
                return m_new, jnp.exp2(m - m_new) * acc + _dot_tn(vs, pt)

            init = (jnp.full((1, tq), -jnp.inf, F32), jnp.zeros((LANES, tq), F32))
            accs.append(lax.fori_loop(0, n_chunks, body, init)[1])
        finish(accs)

    pl.when(bound_max <= SCORE_BOUND_MAX)(bounded)
    pl.when(jnp.logical_not(bound_max <= SCORE_BOUND_MAX))(running_max)


def _attention(q_all, k_all, v_all, tq, tk):
    B, S, _ = q_all.shape
    pairs_per_kv = GQA_GROUP // 2

    def kv_col(p, second):
        return jnp.where(p < GQA_PAIRS, p // pairs_per_kv, GQA_KV_HEADS + 2 * (p - GQA_PAIRS) + second)

    kv_spec = lambda second: pl.BlockSpec((1, S, LANES), lambda b, p, i: (b, 0, kv_col(p, second)))
    return pl.pallas_call(
        functools.partial(_attn_kernel, tk=tk),
        grid=(B, N_PAIRS, S // tq),
        in_specs=[pl.BlockSpec((1, tq, 2 * LANES), lambda b, p, i: (b, i, p)),
                  kv_spec(0), kv_spec(1), kv_spec(0), kv_spec(1)],
        out_specs=pl.BlockSpec((1, tq, LANES), lambda b, p, i: (b, i, p)),
        out_shape=jax.ShapeDtypeStruct((B, S, N_PAIRS * LANES), F32),
        scratch_shapes=[pltpu.VMEM((2, 8, LANES), F32)],
        compiler_params=_cparams(3),
        name="attention",
    )(q_all, k_all, k_all, v_all, v_all)


def _mem_kv_kernel(m_ref, ln_ref, w_ref, o_ref):
    h = (_rms(m_ref[0]) * ln_ref[0]).astype(BF16)
    o_ref[0] = _dot(h, w_ref[0]).astype(BF16)


def _mem_kv(mem, sw, l):
    B, M, D = mem.shape
    ln, w = sw["ln_mem_kv"], sw["w_mem_kv"]
    n = w.shape[2]
    return pl.pallas_call(
        _mem_kv_kernel,
        grid=(B,),
        in_specs=[pl.BlockSpec((1, M, D), lambda b: (b, 0, 0)), _layer_spec(ln, l), _layer_spec(w, l)],
        out_specs=pl.BlockSpec((1, M, n), lambda b: (b, 0, 0)),
        out_shape=jax.ShapeDtypeStruct((B, M, n), BF16),
        compiler_params=_cparams(1),
        name="mem_kv",
    )(mem, ln, w)


def _post_attn_kernel(o_ref, x_ref, on_ref, wo_ref, lnm_ref, wmq_ref, kv_ref, wmo_ref, lnf_ref, wr_ref,
                      x2_ref, h3_ref, aff_ref):
    o = o_ref[0]
    half = o.shape[1] // 2
    merged = (jnp.concatenate([_rms(o[:, :half]), _rms(o[:, half:])], axis=-1) * on_ref[0]).astype(BF16)
    x1 = x_ref[0] + _dot(merged, wo_ref[0])

    h2 = (_rms(x1) * lnm_ref[0]).astype(BF16)
    q = (_dot(h2, wmq_ref[0]) * (MEM_HEAD_DIM ** -0.5 * LOG2E)).astype(BF16)
    kv = kv_ref[0]
    n_mem = MEM_HEADS * MEM_HEAD_DIM
    outs = []
    for hh in range(MEM_HEADS):
        lo, hi = hh * MEM_HEAD_DIM, (hh + 1) * MEM_HEAD_DIM
        s = _dot_nt(q[:, lo:hi], kv[:, lo:hi])
        p = jnp.exp2(s - jnp.max(s, axis=-1, keepdims=True))
        l = jnp.sum(p, axis=-1, keepdims=True)
        outs.append(_dot(p.astype(BF16), kv[:, n_mem + lo:n_mem + hi]) * (1.0 / l))
    oc = jnp.concatenate(outs, axis=-1).astype(BF16)
    x2 = x1 + _dot(oc, wmo_ref[0])
    x2_ref[0] = x2

    h3 = _rms(x2) * lnf_ref[0]
    h3_ref[0] = h3.astype(BF16)
    h_hi = h3.astype(BF16)
    h_lo = (h3 - h_hi.astype(F32)).astype(BF16)
    n_e = wr_ref.shape[1] // 2
    part = _dot_nt(wr_ref[0], h_hi)
    logits = part[:n_e] + part[n_e:] + _dot_nt(wr_ref[0, 0:n_e, :], h_lo)
    e = jnp.exp(logits - jnp.max(logits, axis=0, keepdims=True))
    aff_ref[0] = e * (1.0 / jnp.sum(e, axis=0, keepdims=True))


def _post_attn(o, x, kv_mem, sw, l, ts):
    B, S, D = x.shape
    E = N_EXPERTS
    consts_a = (sw["on"], sw["w_o"], sw["ln_mem"], sw["w_mem_q"])
    consts_b = (sw["w_mem_o"], sw["ln_ffn"], sw["w_router_t"])
    tile = lambda n: pl.BlockSpec((1, ts, n), lambda b, i: (b, i, 0))
    return pl.pallas_call(
        _post_attn_kernel,
        grid=(B, S // ts),
        in_specs=[tile(o.shape[2]), tile(D)] + [_layer_spec(a, l) for a in consts_a]
        + [pl.BlockSpec((1,) + kv_mem.shape[1:], lambda b, i: (b, 0, 0))] + [_layer_spec(a, l) for a in consts_b],
        out_specs=[tile(D), tile(D), pl.BlockSpec((1, E, ts), lambda b, i: (b, 0, i))],
        out_shape=[jax.ShapeDtypeStruct((B, S, D), F32), jax.ShapeDtypeStruct((B, S, D), BF16),
                   jax.ShapeDtypeStruct((B, E, S), F32)],
        compiler_params=_cparams(2),
        name="post_attn",
    )(o, x, *consts_a, kv_mem, *consts_b)


def _select_kernel(aff_ref, slot_ref, pos_ref, *, cap, chunks, seq_rows):
    a = aff_ref[...]
    n_rows = a.shape[0]
    bits = pltpu.bitcast(a, jnp.int32)

    r_i = lax.broadcasted_iota(jnp.int32, (seq_rows, seq_rows), 0)
    c_i = lax.broadcasted_iota(jnp.int32, (seq_rows, seq_rows), 1)
    earlier_chunk = ((r_i // chunks) == (c_i // chunks)) & (c_i < r_i)
    bd_before = jnp.where(earlier_chunk, 1.0, 0.0).astype(BF16)
    l_r = lax.broadcasted_iota(jnp.int32, (LANES, LANES), 0)
    l_c = lax.broadcasted_iota(jnp.int32, (LANES, LANES), 1)
    ones = jnp.ones((LANES, LANES), BF16)
    before = jnp.where(l_r < l_c, 1.0, 0.0).astype(BF16)

    def as01(mask):
        return jnp.where(mask, 1.0, 0.0).astype(BF16)

    def expert_count(x01):
        per_chunk = _dot(x01, ones).reshape(n_rows // chunks, chunks, LANES)
        total = jnp.sum(per_chunk, axis=1, keepdims=True)
        return jnp.broadcast_to(total, per_chunk.shape).reshape(n_rows, LANES)

    def prefix_excl(x01):
        per_chunk = _dot(x01, ones).astype(BF16)
        earlier = [_dot(bd_before, per_chunk[s * seq_rows:(s + 1) * seq_rows]) for s in range(n_rows // seq_rows)]
        return _dot(x01, before) + jnp.concatenate(earlier, axis=0)

    def step(i, theta):
        cand = theta | (jnp.int32(1) << (30 - i))
        cnt = expert_count(as01(bits >= cand))
        return jnp.where(cnt >= cap, cand, theta)

    theta = lax.fori_loop(0, 31, step, jnp.zeros(bits.shape, jnp.int32))
    gt = bits > theta
    eq = bits == theta
    need = cap - expert_count(as01(gt))
    sel = gt | (eq & (prefix_excl(as01(eq)) < need))
    pos = prefix_excl(as01(sel))
    slot_ref[...] = jnp.where(sel, pos, -1.0).astype(jnp.int32)
    pos_ref[...] = pos.astype(jnp.int32)


def _select(aff2, cap, chunks):
    B, seq_rows, _ = aff2.shape
    assert chunks % 8 == 0, "the per-expert reduction reshapes rows into whole sublane tiles"
    n_rows = B * seq_rows
    blk = pl.BlockSpec((n_rows, LANES), lambda i: (0, 0))
    slot, pos = pl.pallas_call(
        functools.partial(_select_kernel, cap=cap, chunks=chunks, seq_rows=seq_rows),
        grid=(1,),
        in_specs=[blk],
        out_specs=[blk, blk],
        out_shape=[jax.ShapeDtypeStruct((n_rows, LANES), jnp.int32)] * 2,
        compiler_params=_cparams(1),
        name="select",
    )(aff2.reshape(n_rows, LANES))
    return slot.reshape(aff2.shape), pos.reshape(aff2.shape)


GATHER_TILE = 2 * LANES
GATHER_WIN = GATHER_TILE + 8
GATHER_WIN_NARROW = LANES
SCATTER_TILE = LANES
SCATTER_WIN = 2 * LANES
SCATTER_WIN_NARROW = 64


def _y_rows(cap):
    return -(-cap // LANES) * LANES + LANES


def _expert_kernel(ps_ref, slot_ref, aff_ref, h_ref, wg_ref, wu_ref, wd_ref, y_ref, xin_ref, gate_ref, *, cap, chunks):
    base = (pl.program_id(0) * N_EXPERTS + pl.program_id(1)) * chunks
    S = h_ref.shape[1]
    n_tiles = S // GATHER_TILE
    bounds = [ps_ref[base + t * (GATHER_TILE // LANES)] for t in range(n_tiles)] + [cap]
    most = bounds[1] - bounds[0]
    for t in range(1, n_tiles):
        most = jnp.maximum(most, bounds[t + 1] - bounds[t])
    xin_ref[...] = jnp.zeros_like(xin_ref)
    gate_ref[...] = jnp.zeros_like(gate_ref)

    def gather(win):
        row = lax.broadcasted_iota(jnp.int32, (win, GATHER_TILE), 0)
        for t in range(n_tiles):
            lo, hi = t * GATHER_TILE, (t + 1) * GATHER_TILE
            start = pl.multiple_of((bounds[t] >> 3) << 3, 8)
            hit = (row + start) == slot_ref[0, 0, :, lo:hi]
            onehot = jnp.where(hit, 1.0, 0.0).astype(BF16)
            xin_ref[pl.ds(start, win), :] += _dot(onehot, h_ref[0, lo:hi, :])
            g = jnp.sum(jnp.where(hit, aff_ref[0, 0, :, lo:hi], 0.0), axis=-1, keepdims=True)
            gate_ref[pl.ds(start, win), :] += jnp.broadcast_to(g, (win, LANES))

    narrow = most <= GATHER_WIN_NARROW - 8
    pl.when(narrow)(functools.partial(gather, GATHER_WIN_NARROW))
    pl.when(jnp.logical_not(narrow))(functools.partial(gather, GATHER_WIN))
    x_in = xin_ref[0:cap, :].astype(BF16)
    a = _dot(x_in, wg_ref[0, 0].astype(BF16))
    u = _dot(x_in, wu_ref[0, 0].astype(BF16))
    hm = (a * (1.0 / (1.0 + jnp.exp(-a))) * u).astype(BF16)
    y_ref[0, 0, 0:cap, :] = (_dot(hm, wd_ref[0, 0].astype(BF16)) * gate_ref[0:cap, 0:1]).astype(BF16)
    y_ref[0, 0, cap:, :] = jnp.zeros((y_ref.shape[2] - cap, y_ref.shape[3]), BF16)


def _experts(pstart, slot_row, aff_row, h3, wg, wu, wd, l, cap, chunks):
    B, S, D = h3.shape
    E = N_EXPERTS
    F = wg.shape[3]
    assert S % GATHER_TILE == 0 and cap % 8 == 0
    row = pl.BlockSpec((1, 1, 1, S), lambda b, e, ps: (b, e, 0, 0))
    return pl.pallas_call(
        functools.partial(_expert_kernel, cap=cap, chunks=chunks),
        grid_spec=pltpu.PrefetchScalarGridSpec(
            num_scalar_prefetch=1,
            grid=(B, E),
            in_specs=[row, row, pl.BlockSpec((1, S, D), lambda b, e, ps: (b, 0, 0)),
                      pl.BlockSpec((1, 1, D, F), lambda b, e, ps: (l, e, 0, 0)),
                      pl.BlockSpec((1, 1, D, F), lambda b, e, ps: (l, e, 0, 0)),
                      pl.BlockSpec((1, 1, F, D), lambda b, e, ps: (l, e, 0, 0))],
            out_specs=pl.BlockSpec((1, 1, _y_rows(cap), D), lambda b, e, ps: (b, e, 0, 0)),
            scratch_shapes=[pltpu.VMEM((cap + GATHER_WIN, D), F32), pltpu.VMEM((cap + GATHER_WIN, LANES), F32)]),
        out_shape=jax.ShapeDtypeStruct((B, E, _y_rows(cap), D), BF16),
        compiler_params=_cparams(2),
        name="experts",
    )(pstart, slot_row, aff_row, h3, wg, wu, wd)


def _combine_kernel(ps_ref, x_ref, slot_ref, y_ref, lnf_ref, o_ref, *, cap, chunks, final):
    tc = x_ref.shape[1]
    n_sub = tc // SCATTER_TILE
    n_e = N_EXPERTS
    b, i = pl.program_id(0), pl.program_id(1)
    y_rows = y_ref.shape[2]
    firsts, most = {}, 0
    for sub in range(n_sub):
        chunk = i * n_sub + sub
        for e in range(n_e):
            idx = (b * n_e + e) * chunks + chunk
            firsts[sub, e] = ps_ref[idx]
            most = jnp.maximum(most, jnp.where(chunk == chunks - 1, cap, ps_ref[idx + 1]) - firsts[sub, e])

    def finish(sub, moe):
        lo, hi = sub * SCATTER_TILE, (sub + 1) * SCATTER_TILE
        acc = x_ref[0, lo:hi, :] + moe
        if final:
            acc = _rms(acc) * lnf_ref[...]
        o_ref[0, lo:hi, :] = acc

    def wide():
        col = lax.broadcasted_iota(jnp.int32, (SCATTER_TILE, SCATTER_WIN), 1)
        for sub in range(n_sub):
            slot_t = slot_ref[0, sub * SCATTER_TILE:(sub + 1) * SCATTER_TILE, :]
            hots, wins = [], []
            for e in range(n_e):
                start = pl.multiple_of(jnp.minimum((firsts[sub, e] >> 7) << 7, y_rows - SCATTER_WIN), LANES)
                hots.append(jnp.where(slot_t[:, e:e + 1] == col + start, 1.0, 0.0).astype(BF16))
                wins.append(y_ref[0, e, pl.ds(start, SCATTER_WIN), :])
            finish(sub, _dot(jnp.concatenate(hots, axis=1), jnp.concatenate(wins, axis=0)))

    def narrow():
        w = SCATTER_WIN_NARROW
        lane = lax.broadcasted_iota(jnp.int32, (1, n_e * w), 1)
        offset = (lane & (w - 1)).astype(F32)
        spread = jnp.where(lax.broadcasted_iota(jnp.int32, (n_e, n_e * w), 0) == (lane >> 6), 1.0, 0.0).astype(BF16)
        e_lane = lax.broadcasted_iota(jnp.int32, (1, n_e), 1)
        for sub in range(n_sub):
            slot_t = slot_ref[0, sub * SCATTER_TILE:(sub + 1) * SCATTER_TILE, :]
            start_row = jnp.zeros((1, n_e), jnp.int32)
            wins = []
            for e in range(n_e):
                start = pl.multiple_of(jnp.minimum((firsts[sub, e] >> 4) << 4, y_rows - w), 16)
                start_row = jnp.where(e_lane == e, start, start_row)
                wins.append(y_ref[0, e, pl.ds(start, w), :])
            local = slot_t - start_row
            local = jnp.where((local >= 0) & (local < w), local, -1).astype(F32).astype(BF16)
            onehot = jnp.where(_dot(local, spread) == offset, 1.0, 0.0).astype(BF16)
            finish(sub, _dot(onehot, jnp.concatenate(wins, axis=0)))

    fits = most <= SCATTER_WIN_NARROW - 15
    pl.when(fits)(narrow)
    pl.when(jnp.logical_not(fits))(wide)


def _combine(pstart, x2, slot_t, y, ln_final, cap, chunks, tc, final):
    B, S, D = x2.shape
    E = N_EXPERTS
    assert tc % SCATTER_TILE == 0 and SCATTER_TILE == LANES and SCATTER_WIN_NARROW == 64
    return pl.pallas_call(
        functools.partial(_combine_kernel, cap=cap, chunks=chunks, final=final),
        grid_spec=pltpu.PrefetchScalarGridSpec(
            num_scalar_prefetch=1,
            grid=(B, S // tc),
            in_specs=[pl.BlockSpec((1, tc, D), lambda b, i, ps: (b, i, 0)),
                      pl.BlockSpec((1, tc, E), lambda b, i, ps: (b, i, 0)),
                      pl.BlockSpec((1,) + y.shape[1:], lambda b, i, ps: (b, 0, 0, 0), pipeline_mode=pl.Buffered(1)),
                      pl.BlockSpec(ln_final.shape, lambda b, i, ps: (0, 0))],
            out_specs=pl.BlockSpec((1, tc, D), lambda b, i, ps: (b, i, 0))),
        out_shape=jax.ShapeDtypeStruct((B, S, D), F32),
        compiler_params=_cparams(2),
        name="combine",
    )(pstart, x2, slot_t, y, ln_final)


def _stacked_weights(p):
    w_in = p["w_in"]
    o1 = GQA_HEADS * GQA_HEAD_DIM
    o2 = o1 + GQA_KV_HEADS * GQA_HEAD_DIM
    o3 = o2 + GQA_KV_HEADS * GQA_HEAD_DIM
    o4 = o3 + MLA_Q_RANK
    o5 = o4 + MLA_KV_RANK
    heads = lambda w, n: w.reshape(w.shape[:-1] + (n, w.shape[-1] // n))
    flat = lambda w: w.reshape(w.shape[:-2] + (-1,))
    wq = flat(_lay_gqa(heads(w_in[..., :o1], GQA_HEADS)))
    wk = flat(_lay_gqa(heads(w_in[..., o1:o2], GQA_KV_HEADS)))
    wv = flat(_lay_v(heads(w_in[..., o2:o3], GQA_KV_HEADS)))
    w_kr = w_in[..., o5:]
    wkr = _lay_mla(jnp.zeros(w_kr.shape[:-1] + (MLA_NOPE_DIM,), F32), w_kr)
    wcat = jnp.concatenate([wq, wk, wv, w_in[..., o3:o4], w_in[..., o4:o5], wkr], axis=-1).astype(BF16)

    wqb = heads(p["w_q_b"], MLA_HEADS)
    wqb = flat(_lay_mla(wqb[..., :MLA_NOPE_DIM], wqb[..., MLA_NOPE_DIM:])).astype(BF16)
    wkvb = heads(p["w_kv_b"], MLA_HEADS)
    k_nope = wkvb[..., :MLA_NOPE_DIM]
    wkb = flat(_lay_mla(k_nope, jnp.zeros(k_nope.shape[:-1] + (MLA_ROPE_DIM,), F32))).astype(BF16)
    wvb = flat(_lay_v(wkvb[..., MLA_NOPE_DIM:])).astype(BF16)

    q_scale = GQA_HEAD_DIM ** -0.5 * LOG2E
    row = lambda v: v[:, None, :]
    return {
        "ln_mix": row(p["ln_mix"]), "wcat": wcat, "wqb": wqb, "wkb": wkb, "wvb": wvb,
        "gq": row(_lay_gqa(p["gqa_q_norm"] * q_scale)), "gk": row(_lay_gqa(p["gqa_k_norm"])),
        "gql": row(p["mla_q_norm"]), "gkv": row(p["mla_kv_norm"]),
        "on": row(jnp.concatenate([p["out_norm_gqa"], p["out_norm_mla"]], axis=-1)),
        "w_o": p["w_o"].astype(BF16), "ln_mem": row(p["ln_mem"]), "ln_mem_kv": row(p["ln_mem_kv"]),
        "w_mem_q": p["w_mem_q"].astype(BF16), "w_mem_kv": p["w_mem_kv"].astype(BF16),
        "w_mem_o": p["w_mem_o"].astype(BF16), "ln_ffn": row(p["ln_ffn"]),
        "w_router_t": _split_hi_lo(jnp.swapaxes(p["w_router"], 1, 2)),
    }


def _split_hi_lo(w):
    hi = w.astype(BF16)
    lo = (w - hi.astype(F32)).astype(BF16)
    return jnp.concatenate([hi, lo], axis=1)


def _tables(seq_len):
    cos_g, sin_g, cos_m, sin_m = _rope_tables(seq_len)
    mq_scale = MLA_QK_DIM ** -0.5 * LOG2E
    one_g = jnp.zeros((GQA_KV_HEADS, LANES), F32).at[:, ONE_LANE].set(1.0)
    one_m = jnp.zeros((MLA_HEADS, LANES), F32).at[:, ONE_LANE].set(1.0)
    return {"cos_g": cos_g, "sin_g": sin_g, "cos_m": cos_m, "sin_m": sin_m,
            "cos_mq": cos_m * mq_scale, "sin_mq": sin_m * mq_scale,
            "one_g": one_g.reshape(1, -1), "one_m": one_m.reshape(1, -1)}


def _pick(n, pref):
    t = min(n, pref)
    assert n % t == 0, (n, t)
    return t


def kernel(x, mem, ln_mix, w_in, gqa_q_norm, gqa_k_norm, mla_q_norm, mla_kv_norm, w_q_b, w_kv_b, out_norm_gqa,
           out_norm_mla, w_o, ln_mem, ln_mem_kv, w_mem_q, w_mem_kv, w_mem_o, ln_ffn, w_router, w_gate, w_up,
           w_down, ln_final):
    p = dict(ln_mix=ln_mix, w_in=w_in, gqa_q_norm=gqa_q_norm, gqa_k_norm=gqa_k_norm, mla_q_norm=mla_q_norm,
             mla_kv_norm=mla_kv_norm, w_q_b=w_q_b, w_kv_b=w_kv_b, out_norm_gqa=out_norm_gqa,
             out_norm_mla=out_norm_mla, w_o=w_o, ln_mem=ln_mem, ln_mem_kv=ln_mem_kv, w_mem_q=w_mem_q,
             w_mem_kv=w_mem_kv, w_mem_o=w_mem_o, ln_ffn=ln_ffn, w_router=w_router, w_gate=w_gate, w_up=w_up,
             w_down=w_down)
    B, S, D = x.shape
    depth = w_in.shape[0]
    E = N_EXPERTS
    assert S % LANES == 0 and S % GRID_W == 0
    cap = EC_CAPACITY_FACTOR * S // E
    chunks = S // LANES
    tabs = _tables(S)
    ln_final2 = ln_final.reshape(1, -1)
    ts_in, ts_post, tq, tk, tc = _pick(S, 256), _pick(S, 1024), _pick(S, 4096), _pick(S, 512), _pick(S, 256)
    sw = _stacked_weights(p)

    for l in range(depth):
        q_all, k_all, v_all = _mixer_in(x, sw, l, tabs, ts_in)
        o = _attention(q_all, k_all, v_all, tq, tk)
        kv_mem = _mem_kv(mem, sw, l)
        x2, h3, aff = _post_attn(o, x, kv_mem, sw, l, ts_post)
        slot, pos = _select(aff.reshape(B, E * chunks, LANES), cap, chunks)
        slot = slot.reshape(B, E, S)
        pstart = jnp.pad(pos[:, :, 0].reshape(B * E * chunks), (0, 1))
        y = _experts(pstart, slot.reshape(B, E, 1, S), aff.reshape(B, E, 1, S), h3, w_gate, w_up, w_down, l,
                     cap, chunks)
        x = _combine(pstart, x2, jnp.swapaxes(slot, 1, 2), y, ln_final2, cap, chunks, tc, final=(l == depth - 1))
    return x
```

```python
import functools
import math

import numpy as np
import jax
import jax.numpy as jnp
from jax import lax
from jax.experimental import pallas as pl
from jax.experimental.pallas import tpu as pltpu

F32 = jnp.float32
BF16 = jnp.bfloat16

GRID_W = 64
ROPE_THETA = 10000.0
EPS = 1e-6
GQA_HEADS = 8
GQA_KV_HEADS = 2
GQA_GROUP = GQA_HEADS // GQA_KV_HEADS
GQA_HEAD_DIM = 64
MLA_HEADS = 8
MLA_Q_RANK = 256
MLA_KV_RANK = 128
MLA_NOPE_DIM = 64
MLA_ROPE_DIM = 32
MLA_V_DIM = 64
MLA_QK_DIM = MLA_NOPE_DIM + MLA_ROPE_DIM
MEM_HEADS = 4
MEM_HEAD_DIM = 128
N_EXPERTS = 16
EC_CAPACITY_FACTOR = 2

LANES = 128
LOG2E = math.log2(math.e)
VMEM_LIMIT = 56 * 1024 * 1024

N_HEADS = GQA_HEADS + MLA_HEADS
N_PAIRS = N_HEADS // 2
GQA_PAIRS = GQA_HEADS // 2


def _cparams(n_axes):
    return pltpu.CompilerParams(dimension_semantics=("arbitrary",) * n_axes, vmem_limit_bytes=VMEM_LIMIT)


def _rms(x, eps=EPS):
    return x * lax.rsqrt(jnp.mean(x * x, axis=-1, keepdims=True) + eps)


def _dot(a, b):
    return jnp.dot(a, b, preferred_element_type=F32)


def _dot_nt(a, b):
    return lax.dot_general(a, b, (((1,), (1,)), ((), ())), preferred_element_type=F32)


V_DIM = 64
ONE_LANE = V_DIM


def _zeros_like_cols(w, n):
    return jnp.zeros(w.shape[:-1] + (n,), w.dtype)


def _lay_gqa(w):
    z = _zeros_like_cols(w, 32)
    return jnp.concatenate([w[..., :32], z, w[..., 32:], z], axis=-1)


def _lay_mla(nope, rope):
    z = _zeros_like_cols(nope, 16)
    return jnp.concatenate([nope[..., :32], rope[..., :16], z, nope[..., 32:], rope[..., 16:], z], axis=-1)


def _lay_v(v):
    return jnp.concatenate([v, _zeros_like_cols(v, LANES - V_DIM)], axis=-1)


def _rope_tables(seq_len):
    rows = seq_len // GRID_W
    row = jnp.repeat(jnp.arange(rows, dtype=F32), GRID_W)
    col = jnp.tile(jnp.arange(GRID_W, dtype=F32), rows)

    def angles(rot_dim):
        axis_dim = rot_dim // 2
        inv_freq = ROPE_THETA ** (-jnp.arange(0, axis_dim, 2, dtype=F32) / axis_dim)
        ang = jnp.concatenate([row[:, None] * inv_freq[None, :], col[:, None] * inv_freq[None, :]], axis=-1)
        return jnp.cos(ang), jnp.sin(ang)

    cg, sg = angles(GQA_HEAD_DIM)
    cm, sm = angles(MLA_ROPE_DIM)
    cos_g = _lay_gqa(jnp.concatenate([cg, cg], axis=-1))
    sin_g = _lay_gqa(jnp.concatenate([-sg, sg], axis=-1))
    one = jnp.ones((seq_len, MLA_NOPE_DIM), F32)
    cos_m = _lay_mla(one, jnp.concatenate([cm, cm], axis=-1))
    sin_m = _lay_mla(0.0 * one, jnp.concatenate([-sm, sm], axis=-1))
    return cos_g, sin_g, cos_m, sin_m


_W_PARTS = (("q_gqa", GQA_HEADS * LANES), ("k_gqa", GQA_KV_HEADS * LANES), ("v_gqa", GQA_KV_HEADS * LANES),
            ("q_lat", MLA_Q_RANK), ("kv_lat", MLA_KV_RANK), ("k_rope", LANES))
_W_STARTS = dict(zip((n for n, _ in _W_PARTS), (int(v) for v in np.cumsum((0,) + tuple(w for _, w in _W_PARTS)))))
_W_WIDTH = dict(_W_PARTS)


def _mixer_in_kernel(x_ref, ln_ref, wcat_ref, wqb_ref, wkb_ref, wvb_ref, gq_ref, gk_ref, gql_ref, gkv_ref,
                     cg_ref, sg_ref, cmq_ref, smq_ref, cmk_ref, smk_ref, oneg_ref, onem_ref,
                     q_ref, k_ref, v_ref):
    n_vg = GQA_KV_HEADS * LANES
    h = (_rms(x_ref[0]) * ln_ref[0]).astype(BF16)
    proj = _dot(h, wcat_ref[0])

    def part(name, j=0, width=None):
        lo = _W_STARTS[name] + j * LANES
        return proj[:, lo:lo + (width or _W_WIDTH[name])]

    cg, sg = cg_ref[...], sg_ref[...]

    def head_norm_rope(blk, gain):
        ss = jnp.sum(blk * blk, axis=-1, keepdims=True) * (1.0 / GQA_HEAD_DIM)
        y = blk * lax.rsqrt(ss + EPS) * gain
        return y * cg + pltpu.roll(y, 64, 1) * sg

    for j in range(GQA_HEADS):
        q_ref[0, :, j * LANES:(j + 1) * LANES] = head_norm_rope(part("q_gqa", j, LANES), gq_ref[0]).astype(BF16)
    for j in range(GQA_KV_HEADS):
        k_ref[0, :, j * LANES:(j + 1) * LANES] = head_norm_rope(part("k_gqa", j, LANES), gk_ref[0]).astype(BF16)
    v_ref[0, :, 0:n_vg] = (part("v_gqa") + oneg_ref[...]).astype(BF16)

    c_q = (_rms(part("q_lat")) * gql_ref[0]).astype(BF16)
    qm = _dot(c_q, wqb_ref[0])
    cmq, smq = cmq_ref[...], smq_ref[...]
    for j in range(MLA_HEADS):
        blk = qm[:, j * LANES:(j + 1) * LANES]
        q_ref[0, :, (GQA_HEADS + j) * LANES:(GQA_HEADS + j + 1) * LANES] = (
            blk * cmq + pltpu.roll(blk, 64, 1) * smq).astype(BF16)

    c_kv = (_rms(part("kv_lat")) * gkv_ref[0]).astype(BF16)
    kn = _dot(c_kv, wkb_ref[0])
    vm = _dot(c_kv, wvb_ref[0]) + onem_ref[...]
    kr = part("k_rope")
    kr = kr * cmk_ref[...] + pltpu.roll(kr, 64, 1) * smk_ref[...]
    for j in range(MLA_HEADS):
        k_ref[0, :, (GQA_KV_HEADS + j) * LANES:(GQA_KV_HEADS + j + 1) * LANES] = (
            kn[:, j * LANES:(j + 1) * LANES] + kr).astype(BF16)
    v_ref[0, :, n_vg:] = vm.astype(BF16)


def _layer_spec(a, l):
    zeros = (0,) * (a.ndim - 1)
    return pl.BlockSpec((1,) + a.shape[1:], lambda *grid_ids: (l,) + zeros)


def _mixer_in(x, sw, l, tabs, ts):
    B, S, D = x.shape
    nq, nk = N_HEADS * LANES, (GQA_KV_HEADS + MLA_HEADS) * LANES
    nv = nk
    full = lambda a: pl.BlockSpec(a.shape, lambda b, i: (0,) * a.ndim)
    tab = pl.BlockSpec((ts, LANES), lambda b, i: (i, 0))
    consts = (sw["ln_mix"], sw["wcat"], sw["wqb"], sw["wkb"], sw["wvb"], sw["gq"], sw["gk"], sw["gql"], sw["gkv"])
    return pl.pallas_call(
        _mixer_in_kernel,
        grid=(B, S // ts),
        in_specs=[pl.BlockSpec((1, ts, D), lambda b, i: (b, i, 0))] + [_layer_spec(a, l) for a in consts]
        + [tab] * 6 + [full(tabs["one_g"]), full(tabs["one_m"])],
        out_specs=[pl.BlockSpec((1, ts, nq), lambda b, i: (b, i, 0)),
                   pl.BlockSpec((1, ts, nk), lambda b, i: (b, i, 0)),
                   pl.BlockSpec((1, ts, nv), lambda b, i: (b, i, 0))],
        out_shape=[jax.ShapeDtypeStruct((B, S, nq), BF16), jax.ShapeDtypeStruct((B, S, nk), BF16),
                   jax.ShapeDtypeStruct((B, S, nv), BF16)],
        compiler_params=_cparams(2),
        name="mixer_in",
    )(x, *consts, tabs["cos_g"], tabs["sin_g"], tabs["cos_mq"], tabs["sin_mq"], tabs["cos_m"], tabs["sin_m"],
      tabs["one_g"], tabs["one_m"])


SCORE_BOUND_MAX = 40.0
BOUND_SLACK = 1.02


def _dot_tn(a, b):
    return lax.dot_general(a, b, (((0,), (0,)), ((), ())), preferred_element_type=F32)


def _attn_kernel(q_ref, ka_ref, kb_ref, va_ref, vb_ref, o_ref, kmax_ref, *, tk):
    tq = q_ref.shape[1]
    n_chunks = ka_ref.shape[1] // tk
    ones8 = jnp.ones((8, LANES), BF16)
    k_refs = (ka_ref, kb_ref)
    v_refs = (va_ref, vb_ref)

    @pl.when(pl.program_id(2) == 0)
    def _():
        for h in range(2):
            kk = k_refs[h][0]
            ksq = _dot_nt(ones8, kk * kk)
            kmax_ref[h] = jnp.broadcast_to(jnp.max(ksq, axis=-1, keepdims=True), (8, LANES))

    qs = (q_ref[0, :, 0:LANES], q_ref[0, :, LANES:2 * LANES])
    bounds = []
    for h in range(2):
        qsq = _dot_nt(ones8, qs[h] * qs[h])[0:1]
        bounds.append(jnp.sqrt(qsq * kmax_ref[h][0:1, 0:1]) * BOUND_SLACK)
    bound_max = jnp.max(jnp.maximum(bounds[0], bounds[1]))

    def finish(accs):
        outs = [a[0:V_DIM] * (1.0 / a[ONE_LANE:ONE_LANE + 1]) for a in accs]
        o_ref[0] = jnp.concatenate(outs, axis=0).T

    def bounded():
        accs = [jnp.zeros((LANES, tq), F32), jnp.zeros((LANES, tq), F32)]
        for c in range(n_chunks):
            for h in range(2):
                ks = k_refs[h][0, c * tk:(c + 1) * tk, :]
                vs = v_refs[h][0, c * tk:(c + 1) * tk, :]
                pt = jnp.exp2(_dot_nt(ks, qs[h]) - bounds[h]).astype(BF16)
                accs[h] = accs[h] + _dot_tn(vs, pt)
        finish(accs)

    def running_max():
        accs = []
        for h in range(2):
            def body(c, carry):
                m, acc = carry
                start = pl.multiple_of(c * tk, tk)
                ks = k_refs[h][0, pl.ds(start, tk), :]
                vs = v_refs[h][0, pl.ds(start, tk), :]
                st = _dot_nt(ks, qs[h])
                m_new = jnp.maximum(m, jnp.max(st, axis=0, keepdims=True))
                pt = jnp.exp2(st - m_new).astype(BF16)
                return m_new, jnp.exp2(m - m_new) * acc + _dot_tn(vs, pt)

            init = (jnp.full((1, tq), -jnp.inf, F32), jnp.zeros((LANES, tq), F32))
            accs.append(lax.fori_loop(0, n_chunks, body, init)[1])
        finish(accs)

    pl.when(bound_max <= SCORE_BOUND_MAX)(bounded)
    pl.when(jnp.logical_not(bound_max <= SCORE_BOUND_MAX))(running_max)


def _attention(q_all, k_all, v_all, tq, tk):
    B, S, _ = q_all.shape
    pairs_per_kv = GQA_GROUP // 2

    def kv_col(p, second):
        return jnp.where(p < GQA_PAIRS, p // pairs_per_kv, GQA_KV_HEADS + 2 * (p - GQA_PAIRS) + second)

    kv_spec = lambda second: pl.BlockSpec((1, S, LANES), lambda b, p, i: (b, 0, kv_col(p, second)))
    return pl.pallas_call(
        functools.partial(_attn_kernel, tk=tk),
        grid=(B, N_PAIRS, S // tq),
        in_specs=[pl.BlockSpec((1, tq, 2 * LANES), lambda b, p, i: (b, i, p)),
                  kv_spec(0), kv_spec(1), kv_spec(0), kv_spec(1)],
        out_specs=pl.BlockSpec((1, tq, LANES), lambda b, p, i: (b, i, p)),
        out_shape=jax.ShapeDtypeStruct((B, S, N_PAIRS * LANES), F32),
        scratch_shapes=[pltpu.VMEM((2, 8, LANES), F32)],
        compiler_params=_cparams(3),
        name="attention",
    )(q_all, k_all, k_all, v_all, v_all)


def _mem_kv_kernel(m_ref, ln_ref, w_ref, o_ref):
    h = (_rms(m_ref[0]) * ln_ref[0]).astype(BF16)
    o_ref[0] = _dot(h, w_ref[0]).astype(BF16)


def _mem_kv(mem, sw, l):
    B, M, D = mem.shape
    ln, w = sw["ln_mem_kv"], sw["w_mem_kv"]
    n = w.shape[2]
    return pl.pallas_call(
        _mem_kv_kernel,
        grid=(B,),
        in_specs=[pl.BlockSpec((1, M, D), lambda b: (b, 0, 0)), _layer_spec(ln, l), _layer_spec(w, l)],
        out_specs=pl.BlockSpec((1, M, n), lambda b: (b, 0, 0)),
        out_shape=jax.ShapeDtypeStruct((B, M, n), BF16),
        compiler_params=_cparams(1),
        name="mem_kv",
    )(mem, ln, w)


def _post_attn_kernel(o_ref, x_ref, on_ref, wo_ref, lnm_ref, wmq_ref, kv_ref, wmo_ref, lnf_ref, wr_ref,
                      x2_ref, h3_ref, aff_ref):
    o = o_ref[0]
    half = o.shape[1] // 2
    merged = (jnp.concatenate([_rms(o[:, :half]), _rms(o[:, half:])], axis=-1) * on_ref[0]).astype(BF16)
    x1 = x_ref[0] + _dot(merged, wo_ref[0])

    h2 = (_rms(x1) * lnm_ref[0]).astype(BF16)
    q = (_dot(h2, wmq_ref[0]) * (MEM_HEAD_DIM ** -0.5 * LOG2E)).astype(BF16)
    kv = kv_ref[0]
    n_mem = MEM_HEADS * MEM_HEAD_DIM
    outs = []
    for hh in range(MEM_HEADS):
        lo, hi = hh * MEM_HEAD_DIM, (hh + 1) * MEM_HEAD_DIM
        s = _dot_nt(q[:, lo:hi], kv[:, lo:hi])
        p = jnp.exp2(s - jnp.max(s, axis=-1, keepdims=True))
        l = jnp.sum(p, axis=-1, keepdims=True)
        outs.append(_dot(p.astype(BF16), kv[:, n_mem + lo:n_mem + hi]) * (1.0 / l))
    oc = jnp.concatenate(outs, axis=-1).astype(BF16)
    x2 = x1 + _dot(oc, wmo_ref[0])
    x2_ref[0] = x2

    h3 = _rms(x2) * lnf_ref[0]
    h3_ref[0] = h3.astype(BF16)
    h_hi = h3.astype(BF16)
    h_lo = (h3 - h_hi.astype(F32)).astype(BF16)
    n_e = wr_ref.shape[1] // 2
    part = _dot_nt(wr_ref[0], h_hi)
    logits = part[:n_e] + part[n_e:] + _dot_nt(wr_ref[0, 0:n_e, :], h_lo)
    e = jnp.exp(logits - jnp.max(logits, axis=0, keepdims=True))
    aff_ref[0] = e * (1.0 / jnp.sum(e, axis=0, keepdims=True))


def _post_attn(o, x, kv_mem, sw, l, ts):
    B, S, D = x.shape
    E = N_EXPERTS
    consts_a = (sw["on"], sw["w_o"], sw["ln_mem"], sw["w_mem_q"])
    consts_b = (sw["w_mem_o"], sw["ln_ffn"], sw["w_router_t"])
    tile = lambda n: pl.BlockSpec((1, ts, n), lambda b, i: (b, i, 0))
    return pl.pallas_call(
        _post_attn_kernel,
        grid=(B, S // ts),
        in_specs=[tile(o.shape[2]), tile(D)] + [_layer_spec(a, l) for a in consts_a]
        + [pl.BlockSpec((1,) + kv_mem.shape[1:], lambda b, i: (b, 0, 0))] + [_layer_spec(a, l) for a in consts_b],
        out_specs=[tile(D), tile(D), pl.BlockSpec((1, E, ts), lambda b, i: (b, 0, i))],
        out_shape=[jax.ShapeDtypeStruct((B, S, D), F32), jax.ShapeDtypeStruct((B, S, D), BF16),
                   jax.ShapeDtypeStruct((B, E, S), F32)],
        compiler_params=_cparams(2),
        name="post_attn",
    )(o, x, *consts_a, kv_mem, *consts_b)


def _select_kernel(aff_ref, slot_ref, pos_ref, *, cap, chunks, seq_rows):
    a = aff_ref[...]
    n_rows = a.shape[0]
    bits = pltpu.bitcast(a, jnp.int32)

    r_i = lax.broadcasted_iota(jnp.int32, (seq_rows, seq_rows), 0)
    c_i = lax.broadcasted_iota(jnp.int32, (seq_rows, seq_rows), 1)
    earlier_chunk = ((r_i // chunks) == (c_i // chunks)) & (c_i < r_i)
    bd_before = jnp.where(earlier_chunk, 1.0, 0.0).astype(BF16)
    l_r = lax.broadcasted_iota(jnp.int32, (LANES, LANES), 0)
    l_c = lax.broadcasted_iota(jnp.int32, (LANES, LANES), 1)
    ones = jnp.ones((LANES, LANES), BF16)
    before = jnp.where(l_r < l_c, 1.0, 0.0).astype(BF16)

    def as01(mask):
        return jnp.where(mask, 1.0, 0.0).astype(BF16)

    def expert_count(x01):
        per_chunk = _dot(x01, ones).reshape(n_rows // chunks, chunks, LANES)
        total = jnp.sum(per_chunk, axis=1, keepdims=True)
        return jnp.broadcast_to(total, per_chunk.shape).reshape(n_rows, LANES)

    def prefix_excl(x01):
        per_chunk = _dot(x01, ones).astype(BF16)
        earlier = [_dot(bd_before, per_chunk[s * seq_rows:(s + 1) * seq_rows]) for s in range(n_rows // seq_rows)]
        return _dot(x01, before) + jnp.concatenate(earlier, axis=0)

    def step(i, theta):
        cand = theta | (jnp.int32(1) << (30 - i))
        cnt = expert_count(as01(bits >= cand))
        return jnp.where(cnt >= cap, cand, theta)

    theta = lax.fori_loop(0, 31, step, jnp.zeros(bits.shape, jnp.int32))
    gt = bits > theta
    eq = bits == theta
    need = cap - expert_count(as01(gt))
    sel = gt | (eq & (prefix_excl(as01(eq)) < need))
    pos = prefix_excl(as01(sel))
    slot_ref[...] = jnp.where(sel, pos, -1.0).astype(jnp.int32)
    pos_ref[...] = pos.astype(jnp.int32)


def _select(aff2, cap, chunks):
    B, seq_rows, _ = aff2.shape
    assert chunks % 8 == 0, "the per-expert reduction reshapes rows into whole sublane tiles"
    n_rows = B * seq_rows
    blk = pl.BlockSpec((n_rows, LANES), lambda i: (0, 0))
    slot, pos = pl.pallas_call(
        functools.partial(_select_kernel, cap=cap, chunks=chunks, seq_rows=seq_rows),
        grid=(1,),
        in_specs=[blk],
        out_specs=[blk, blk],
        out_shape=[jax.ShapeDtypeStruct((n_rows, LANES), jnp.int32)] * 2,
        compiler_params=_cparams(1),
        name="select",
    )(aff2.reshape(n_rows, LANES))
    return slot.reshape(aff2.shape), pos.reshape(aff2.shape)


GATHER_TILE = 2 * LANES
GATHER_WIN = GATHER_TILE + 8
GATHER_WIN_NARROW = LANES
SCATTER_TILE = LANES
SCATTER_WIN = 2 * LANES
SCATTER_WIN_NARROW = 64


def _y_rows(cap):
    return -(-cap // LANES) * LANES + LANES


def _expert_kernel(ps_ref, slot_ref, aff_ref, h_ref, wg_ref, wu_ref, wd_ref, y_ref, xin_ref, gate_ref, *, cap, chunks):
    base = (pl.program_id(0) * N_EXPERTS + pl.program_id(1)) * chunks
    S = h_ref.shape[1]
    n_tiles = S // GATHER_TILE
    bounds = [ps_ref[base + t * (GATHER_TILE // LANES)] for t in range(n_tiles)] + [cap]
    most = bounds[1] - bounds[0]
    for t in range(1, n_tiles):
        most = jnp.maximum(most, bounds[t + 1] - bounds[t])
    xin_ref[...] = jnp.zeros_like(xin_ref)
    gate_ref[...] = jnp.zeros_like(gate_ref)

    def run(win):
        row = lax.broadcasted_iota(jnp.int32, (win, GATHER_TILE), 0)
        for t in range(n_tiles):
            lo, hi = t * GATHER_TILE, (t + 1) * GATHER_TILE
            start = pl.multiple_of((bounds[t] >> 3) << 3, 8)
            hit = (row + start) == slot_ref[0, 0, :, lo:hi]
            onehot = jnp.where(hit, 1.0, 0.0).astype(BF16)
            xin_ref[pl.ds(start, win), :] += _dot(onehot, h_ref[0, lo:hi, :])
            g = jnp.sum(jnp.where(hit, aff_ref[0, 0, :, lo:hi], 0.0), axis=-1, keepdims=True)
            gate_ref[pl.ds(start, win), :] += jnp.broadcast_to(g, (win, LANES))
        x_in = xin_ref[0:cap, :].astype(BF16)
        a = _dot(x_in, wg_ref[0, 0].astype(BF16))
        u = _dot(x_in, wu_ref[0, 0].astype(BF16))
        hm = (a * (1.0 / (1.0 + jnp.exp(-a))) * u).astype(BF16)
        y_ref[0, 0, 0:cap, :] = (_dot(hm, wd_ref[0, 0].astype(BF16)) * gate_ref[0:cap, 0:1]).astype(BF16)
        y_ref[0, 0, cap:, :] = jnp.zeros((y_ref.shape[2] - cap, y_ref.shape[3]), BF16)

    narrow = most <= GATHER_WIN_NARROW - 8
    pl.when(narrow)(functools.partial(run, GATHER_WIN_NARROW))
    pl.when(jnp.logical_not(narrow))(functools.partial(run, GATHER_WIN))


def _experts(pstart, slot_row, aff_row, h3, wg, wu, wd, l, cap, chunks):
    B, S, D = h3.shape
    E = N_EXPERTS
    F = wg.shape[3]
    assert S % GATHER_TILE == 0 and cap % 8 == 0
    row = pl.BlockSpec((1, 1, 1, S), lambda b, e, ps: (b, e, 0, 0))
    return pl.pallas_call(
        functools.partial(_expert_kernel, cap=cap, chunks=chunks),
        grid_spec=pltpu.PrefetchScalarGridSpec(
            num_scalar_prefetch=1,
            grid=(B, E),
            in_specs=[row, row, pl.BlockSpec((1, S, D), lambda b, e, ps: (b, 0, 0)),
                      pl.BlockSpec((1, 1, D, F), lambda b, e, ps: (l, e, 0, 0)),
                      pl.BlockSpec((1, 1, D, F), lambda b, e, ps: (l, e, 0, 0)),
                      pl.BlockSpec((1, 1, F, D), lambda b, e, ps: (l, e, 0, 0))],
            out_specs=pl.BlockSpec((1, 1, _y_rows(cap), D), lambda b, e, ps: (b, e, 0, 0)),
            scratch_shapes=[pltpu.VMEM((cap + GATHER_WIN, D), F32), pltpu.VMEM((cap + GATHER_WIN, LANES), F32)]),
        out_shape=jax.ShapeDtypeStruct((B, E, _y_rows(cap), D), BF16),
        compiler_params=_cparams(2),
        name="experts",
    )(pstart, slot_row, aff_row, h3, wg, wu, wd)


def _combine_kernel(ps_ref, x_ref, slot_ref, y_ref, lnf_ref, o_ref, *, cap, chunks, final):
    tc = x_ref.shape[1]
    n_sub = tc // SCATTER_TILE
    n_e = N_EXPERTS
    b, i = pl.program_id(0), pl.program_id(1)
    y_rows = y_ref.shape[2]
    firsts, most = {}, 0
    for sub in range(n_sub):
        chunk = i * n_sub + sub
        for e in range(n_e):
            idx = (b * n_e + e) * chunks + chunk
            firsts[sub, e] = ps_ref[idx]
            most = jnp.maximum(most, jnp.where(chunk == chunks - 1, cap, ps_ref[idx + 1]) - firsts[sub, e])

    def finish(sub, moe):
        lo, hi = sub * SCATTER_TILE, (sub + 1) * SCATTER_TILE
        acc = x_ref[0, lo:hi, :] + moe
        if final:
            acc = _rms(acc) * lnf_ref[...]
        o_ref[0, lo:hi, :] = acc

    def wide():
        col = lax.broadcasted_iota(jnp.int32, (SCATTER_TILE, SCATTER_WIN), 1)
        for sub in range(n_sub):
            slot_t = slot_ref[0, sub * SCATTER_TILE:(sub + 1) * SCATTER_TILE, :]
            hots, wins = [], []
            for e in range(n_e):
                start = pl.multiple_of(jnp.minimum((firsts[sub, e] >> 7) << 7, y_rows - SCATTER_WIN), LANES)
                hots.append(jnp.where(slot_t[:, e:e + 1] == col + start, 1.0, 0.0).astype(BF16))
                wins.append(y_ref[0, e, pl.ds(start, SCATTER_WIN), :])
            finish(sub, _dot(jnp.concatenate(hots, axis=1), jnp.concatenate(wins, axis=0)))

    def narrow():
        w = SCATTER_WIN_NARROW
        lane = lax.broadcasted_iota(jnp.int32, (1, n_e * w), 1)
        offset = (lane & (w - 1)).astype(F32)
        spread = jnp.where(lax.broadcasted_iota(jnp.int32, (n_e, n_e * w), 0) == (lane >> 6), 1.0, 0.0).astype(BF16)
        e_lane = lax.broadcasted_iota(jnp.int32, (1, n_e), 1)
        for sub in range(n_sub):
            slot_t = slot_ref[0, sub * SCATTER_TILE:(sub + 1) * SCATTER_TILE, :]
            start_row = jnp.zeros((1, n_e), jnp.int32)
            wins = []
            for e in range(n_e):
                start = pl.multiple_of(jnp.minimum((firsts[sub, e] >> 4) << 4, y_rows - w), 16)
                start_row = jnp.where(e_lane == e, start, start_row)
                wins.append(y_ref[0, e, pl.ds(start, w), :])
            local = slot_t - start_row
            local = jnp.where((local >= 0) & (local < w), local, -1).astype(F32).astype(BF16)
            onehot = jnp.where(_dot(local, spread) == offset, 1.0, 0.0).astype(BF16)
            finish(sub, _dot(onehot, jnp.concatenate(wins, axis=0)))

    fits = most <= SCATTER_WIN_NARROW - 15
    pl.when(fits)(narrow)
    pl.when(jnp.logical_not(fits))(wide)


def _combine(pstart, x2, slot_t, y, ln_final, cap, chunks, tc, final):
    B, S, D = x2.shape
    E = N_EXPERTS
    assert tc % SCATTER_TILE == 0 and SCATTER_TILE == LANES and SCATTER_WIN_NARROW == 64
    return pl.pallas_call(
        functools.partial(_combine_kernel, cap=cap, chunks=chunks, final=final),
        grid_spec=pltpu.PrefetchScalarGridSpec(
            num_scalar_prefetch=1,
            grid=(B, S // tc),
            in_specs=[pl.BlockSpec((1, tc, D), lambda b, i, ps: (b, i, 0)),
                      pl.BlockSpec((1, tc, E), lambda b, i, ps: (b, i, 0)),
                      pl.BlockSpec((1,) + y.shape[1:], lambda b, i, ps: (b, 0, 0, 0), pipeline_mode=pl.Buffered(1)),
                      pl.BlockSpec(ln_final.shape, lambda b, i, ps: (0, 0))],
            out_specs=pl.BlockSpec((1, tc, D), lambda b, i, ps: (b, i, 0))),
        out_shape=jax.ShapeDtypeStruct((B, S, D), F32),
        compiler_params=_cparams(2),
        name="combine",
    )(pstart, x2, slot_t, y, ln_final)


def _stacked_weights(p):
    w_in = p["w_in"]
    o1 = GQA_HEADS * GQA_HEAD_DIM
    o2 = o1 + GQA_KV_HEADS * GQA_HEAD_DIM
    o3 = o2 + GQA_KV_HEADS * GQA_HEAD_DIM
    o4 = o3 + MLA_Q_RANK
    o5 = o4 + MLA_KV_RANK
    heads = lambda w, n: w.reshape(w.shape[:-1] + (n, w.shape[-1] // n))
    flat = lambda w: w.reshape(w.shape[:-2] + (-1,))
    wq = flat(_lay_gqa(heads(w_in[..., :o1], GQA_HEADS)))
    wk = flat(_lay_gqa(heads(w_in[..., o1:o2], GQA_KV_HEADS)))
    wv = flat(_lay_v(heads(w_in[..., o2:o3], GQA_KV_HEADS)))
    w_kr = w_in[..., o5:]
    wkr = _lay_mla(jnp.zeros(w_kr.shape[:-1] + (MLA_NOPE_DIM,), F32), w_kr)
    blocks = {"q_lat": w_in[..., o3:o4], "kv_lat": w_in[..., o4:o5], "k_rope": wkr, "q_gqa": wq, "k_gqa": wk, "v_gqa": wv}
    wcat = jnp.concatenate([blocks[name] for name, _ in _W_PARTS], axis=-1).astype(BF16)

    wqb = heads(p["w_q_b"], MLA_HEADS)
    wqb = flat(_lay_mla(wqb[..., :MLA_NOPE_DIM], wqb[..., MLA_NOPE_DIM:])).astype(BF16)
    wkvb = heads(p["w_kv_b"], MLA_HEADS)
    k_nope = wkvb[..., :MLA_NOPE_DIM]
    wkb = flat(_lay_mla(k_nope, jnp.zeros(k_nope.shape[:-1] + (MLA_ROPE_DIM,), F32))).astype(BF16)
    wvb = flat(_lay_v(wkvb[..., MLA_NOPE_DIM:])).astype(BF16)

    q_scale = GQA_HEAD_DIM ** -0.5 * LOG2E
    row = lambda v: v[:, None, :]
    return {
        "ln_mix": row(p["ln_mix"]), "wcat": wcat, "wqb": wqb, "wkb": wkb, "wvb": wvb,
        "gq": row(_lay_gqa(p["gqa_q_norm"] * q_scale)), "gk": row(_lay_gqa(p["gqa_k_norm"])),
        "gql": row(p["mla_q_norm"]), "gkv": row(p["mla_kv_norm"]),
        "on": row(jnp.concatenate([p["out_norm_gqa"], p["out_norm_mla"]], axis=-1)),
        "w_o": p["w_o"].astype(BF16), "ln_mem": row(p["ln_mem"]), "ln_mem_kv": row(p["ln_mem_kv"]),
        "w_mem_q": p["w_mem_q"].astype(BF16), "w_mem_kv": p["w_mem_kv"].astype(BF16),
        "w_mem_o": p["w_mem_o"].astype(BF16), "ln_ffn": row(p["ln_ffn"]),
        "w_router_t": _split_hi_lo(jnp.swapaxes(p["w_router"], 1, 2)),
    }


def _split_hi_lo(w):
    hi = w.astype(BF16)
    lo = (w - hi.astype(F32)).astype(BF16)
    return jnp.concatenate([hi, lo], axis=1)


def _tables(seq_len):
    cos_g, sin_g, cos_m, sin_m = _rope_tables(seq_len)
    mq_scale = MLA_QK_DIM ** -0.5 * LOG2E
    one_g = jnp.zeros((GQA_KV_HEADS, LANES), F32).at[:, ONE_LANE].set(1.0)
    one_m = jnp.zeros((MLA_HEADS, LANES), F32).at[:, ONE_LANE].set(1.0)
    return {"cos_g": cos_g, "sin_g": sin_g, "cos_m": cos_m, "sin_m": sin_m,
            "cos_mq": cos_m * mq_scale, "sin_mq": sin_m * mq_scale,
            "one_g": one_g.reshape(1, -1), "one_m": one_m.reshape(1, -1)}


def _pick(n, pref):
    t = min(n, pref)
    assert n % t == 0, (n, t)
    return t


def kernel(x, mem, ln_mix, w_in, gqa_q_norm, gqa_k_norm, mla_q_norm, mla_kv_norm, w_q_b, w_kv_b, out_norm_gqa,
           out_norm_mla, w_o, ln_mem, ln_mem_kv, w_mem_q, w_mem_kv, w_mem_o, ln_ffn, w_router, w_gate, w_up,
           w_down, ln_final):
    p = dict(ln_mix=ln_mix, w_in=w_in, gqa_q_norm=gqa_q_norm, gqa_k_norm=gqa_k_norm, mla_q_norm=mla_q_norm,
             mla_kv_norm=mla_kv_norm, w_q_b=w_q_b, w_kv_b=w_kv_b, out_norm_gqa=out_norm_gqa,
             out_norm_mla=out_norm_mla, w_o=w_o, ln_mem=ln_mem, ln_mem_kv=ln_mem_kv, w_mem_q=w_mem_q,
             w_mem_kv=w_mem_kv, w_mem_o=w_mem_o, ln_ffn=ln_ffn, w_router=w_router, w_gate=w_gate, w_up=w_up,
             w_down=w_down)
    B, S, D = x.shape
    depth = w_in.shape[0]
    E = N_EXPERTS
    assert S % LANES == 0 and S % GRID_W == 0
    cap = EC_CAPACITY_FACTOR * S // E
    chunks = S // LANES
    tabs = _tables(S)
    ln_final2 = ln_final.reshape(1, -1)
    ts_in, ts_post, tq, tk, tc = _pick(S, 256), _pick(S, 1024), _pick(S, 4096), _pick(S, 512), _pick(S, 1024)
    sw = _stacked_weights(p)

    for l in range(depth):
        q_all, k_all, v_all = _mixer_in(x, sw, l, tabs, ts_in)
        o = _attention(q_all, k_all, v_all, tq, tk)
        kv_mem = _mem_kv(mem, sw, l)
        x2, h3, aff = _post_attn(o, x, kv_mem, sw, l, ts_post)
        slot, pos = _select(aff.reshape(B, E * chunks, LANES), cap, chunks)
        slot = slot.reshape(B, E, S)
        pstart = jnp.pad(pos[:, :, 0].reshape(B * E * chunks), (0, 1))
        y = _experts(pstart, slot.reshape(B, E, 1, S), aff.reshape(B, E, 1, S), h3, w_gate, w_up, w_down, l,
                     cap, chunks)
        x = _combine(pstart, x2, jnp.swapaxes(slot, 1, 2), y, ln_final2, cap, chunks, tc, final=(l == depth - 1))
    return x
```

```python
import functools
import math

import numpy as np
import jax
import jax.numpy as jnp
from jax import lax
from jax.experimental import pallas as pl
from jax.experimental.pallas import tpu as pltpu

F32 = jnp.float32
BF16 = jnp.bfloat16

GRID_W = 64
ROPE_THETA = 10000.0
EPS = 1e-6
GQA_HEADS = 8
GQA_KV_HEADS = 2
GQA_GROUP = GQA_HEADS // GQA_KV_HEADS
GQA_HEAD_DIM = 64
MLA_HEADS = 8
MLA_Q_RANK = 256
MLA_KV_RANK = 128
MLA_NOPE_DIM = 64
MLA_ROPE_DIM = 32
MLA_V_DIM = 64
MLA_QK_DIM = MLA_NOPE_DIM + MLA_ROPE_DIM
MEM_HEADS = 4
MEM_HEAD_DIM = 128
N_EXPERTS = 16
EC_CAPACITY_FACTOR = 2

LANES = 128
LOG2E = math.log2(math.e)
VMEM_LIMIT = 56 * 1024 * 1024

N_HEADS = GQA_HEADS + MLA_HEADS
N_PAIRS = N_HEADS // 2
GQA_PAIRS = GQA_HEADS // 2


def _cparams(n_axes):
    return pltpu.CompilerParams(dimension_semantics=("arbitrary",) * n_axes, vmem_limit_bytes=VMEM_LIMIT)


def _rms(x, eps=EPS):
    return x * lax.rsqrt(jnp.mean(x * x, axis=-1, keepdims=True) + eps)


def _dot(a, b):
    return jnp.dot(a, b, preferred_element_type=F32)


def _dot_nt(a, b):
    return lax.dot_general(a, b, (((1,), (1,)), ((), ())), preferred_element_type=F32)


V_DIM = 64
ONE_LANE = V_DIM


def _zeros_like_cols(w, n):
    return jnp.zeros(w.shape[:-1] + (n,), w.dtype)


def _lay_gqa(w):
    z = _zeros_like_cols(w, 32)
    return jnp.concatenate([w[..., :32], z, w[..., 32:], z], axis=-1)


def _lay_mla(nope, rope):
    z = _zeros_like_cols(nope, 16)
    return jnp.concatenate([nope[..., :32], rope[..., :16], z, nope[..., 32:], rope[..., 16:], z], axis=-1)


def _lay_v(v):
    return jnp.concatenate([v, _zeros_like_cols(v, LANES - V_DIM)], axis=-1)


def _rope_tables(seq_len):
    rows = seq_len // GRID_W
    row = jnp.repeat(jnp.arange(rows, dtype=F32), GRID_W)
    col = jnp.tile(jnp.arange(GRID_W, dtype=F32), rows)

    def angles(rot_dim):
        axis_dim = rot_dim // 2
        inv_freq = ROPE_THETA ** (-jnp.arange(0, axis_dim, 2, dtype=F32) / axis_dim)
        ang = jnp.concatenate([row[:, None] * inv_freq[None, :], col[:, None] * inv_freq[None, :]], axis=-1)
        return jnp.cos(ang), jnp.sin(ang)

    cg, sg = angles(GQA_HEAD_DIM)
    cm, sm = angles(MLA_ROPE_DIM)
    cos_g = _lay_gqa(jnp.concatenate([cg, cg], axis=-1))
    sin_g = _lay_gqa(jnp.concatenate([-sg, sg], axis=-1))
    one = jnp.ones((seq_len, MLA_NOPE_DIM), F32)
    cos_m = _lay_mla(one, jnp.concatenate([cm, cm], axis=-1))
    sin_m = _lay_mla(0.0 * one, jnp.concatenate([-sm, sm], axis=-1))
    return cos_g, sin_g, cos_m, sin_m


_W_PARTS = (("q_gqa", GQA_HEADS * LANES), ("k_gqa", GQA_KV_HEADS * LANES), ("v_gqa", GQA_KV_HEADS * LANES),
            ("q_lat", MLA_Q_RANK), ("kv_lat", MLA_KV_RANK), ("k_rope", LANES))
_W_STARTS = dict(zip((n for n, _ in _W_PARTS), (int(v) for v in np.cumsum((0,) + tuple(w for _, w in _W_PARTS)))))
_W_WIDTH = dict(_W_PARTS)


def _mixer_in_kernel(x_ref, ln_ref, wcat_ref, wqb_ref, wkb_ref, wvb_ref, gq_ref, gk_ref, gql_ref, gkv_ref,
                     cg_ref, sg_ref, cmq_ref, smq_ref, cmk_ref, smk_ref, oneg_ref, onem_ref,
                     q_ref, k_ref, v_ref, *, sub):
    n_vg = GQA_KV_HEADS * LANES

    def row_block(r, carry):
        rows = pl.ds(pl.multiple_of(r * sub, sub), sub)
        h = (_rms(x_ref[0, rows, :]) * ln_ref[0]).astype(BF16)
        proj = _dot(h, wcat_ref[0])

        def part(name, j=0, width=None):
            lo = _W_STARTS[name] + j * LANES
            return proj[:, lo:lo + (width or _W_WIDTH[name])]

        cg, sg = cg_ref[rows, :], sg_ref[rows, :]

        def head_norm_rope(blk, gain):
            ss = jnp.sum(blk * blk, axis=-1, keepdims=True) * (1.0 / GQA_HEAD_DIM)
            y = blk * lax.rsqrt(ss + EPS) * gain
            return y * cg + pltpu.roll(y, 64, 1) * sg

        for j in range(GQA_HEADS):
            q_ref[0, rows, j * LANES:(j + 1) * LANES] = head_norm_rope(part("q_gqa", j, LANES), gq_ref[0]).astype(BF16)
        for j in range(GQA_KV_HEADS):
            k_ref[0, rows, j * LANES:(j + 1) * LANES] = head_norm_rope(part("k_gqa", j, LANES), gk_ref[0]).astype(BF16)
        v_ref[0, rows, 0:n_vg] = (part("v_gqa") + oneg_ref[...]).astype(BF16)

        c_q = (_rms(part("q_lat")) * gql_ref[0]).astype(BF16)
        qm = _dot(c_q, wqb_ref[0])
        cmq, smq = cmq_ref[rows, :], smq_ref[rows, :]
        for j in range(MLA_HEADS):
            blk = qm[:, j * LANES:(j + 1) * LANES]
            q_ref[0, rows, (GQA_HEADS + j) * LANES:(GQA_HEADS + j + 1) * LANES] = (
                blk * cmq + pltpu.roll(blk, 64, 1) * smq).astype(BF16)

        c_kv = (_rms(part("kv_lat")) * gkv_ref[0]).astype(BF16)
        kn = _dot(c_kv, wkb_ref[0])
        vm = _dot(c_kv, wvb_ref[0]) + onem_ref[...]
        kr = part("k_rope")
        kr = kr * cmk_ref[rows, :] + pltpu.roll(kr, 64, 1) * smk_ref[rows, :]
        for j in range(MLA_HEADS):
            k_ref[0, rows, (GQA_KV_HEADS + j) * LANES:(GQA_KV_HEADS + j + 1) * LANES] = (
                kn[:, j * LANES:(j + 1) * LANES] + kr).astype(BF16)
        v_ref[0, rows, n_vg:] = vm.astype(BF16)
        return carry

    lax.fori_loop(0, x_ref.shape[1] // sub, row_block, 0)


def _layer_spec(a, l):
    zeros = (0,) * (a.ndim - 1)
    return pl.BlockSpec((1,) + a.shape[1:], lambda *grid_ids: (l,) + zeros)


MIXER_SUB = 256


def _mixer_in(x, sw, l, tabs, ts):
    B, S, D = x.shape
    sub = min(ts, MIXER_SUB)
    assert ts % sub == 0
    nq, nk = N_HEADS * LANES, (GQA_KV_HEADS + MLA_HEADS) * LANES
    nv = nk
    full = lambda a: pl.BlockSpec(a.shape, lambda b, i: (0,) * a.ndim)
    tab = pl.BlockSpec((ts, LANES), lambda b, i: (i, 0))
    consts = (sw["ln_mix"], sw["wcat"], sw["wqb"], sw["wkb"], sw["wvb"], sw["gq"], sw["gk"], sw["gql"], sw["gkv"])
    return pl.pallas_call(
        functools.partial(_mixer_in_kernel, sub=sub),
        grid=(B, S // ts),
        in_specs=[pl.BlockSpec((1, ts, D), lambda b, i: (b, i, 0))] + [_layer_spec(a, l) for a in consts]
        + [tab] * 6 + [full(tabs["one_g"]), full(tabs["one_m"])],
        out_specs=[pl.BlockSpec((1, ts, nq), lambda b, i: (b, i, 0)),
                   pl.BlockSpec((1, ts, nk), lambda b, i: (b, i, 0)),
                   pl.BlockSpec((1, ts, nv), lambda b, i: (b, i, 0))],
        out_shape=[jax.ShapeDtypeStruct((B, S, nq), BF16), jax.ShapeDtypeStruct((B, S, nk), BF16),
                   jax.ShapeDtypeStruct((B, S, nv), BF16)],
        compiler_params=_cparams(2),
        name="mixer_in",
    )(x, *consts, tabs["cos_g"], tabs["sin_g"], tabs["cos_mq"], tabs["sin_mq"], tabs["cos_m"], tabs["sin_m"],
      tabs["one_g"], tabs["one_m"])


SCORE_BOUND_MAX = 40.0
BOUND_SLACK = 1.02


def _dot_tn(a, b):
    return lax.dot_general(a, b, (((0,), (0,)), ((), ())), preferred_element_type=F32)


def _attn_kernel(q_ref, ka_ref, kb_ref, va_ref, vb_ref, o_ref, kmax_ref, *, tk):
    tq = q_ref.shape[1]
    n_chunks = ka_ref.shape[1] // tk
    ones8 = jnp.ones((8, LANES), BF16)
    k_refs = (ka_ref, kb_ref)
    v_refs = (va_ref, vb_ref)

    def largest_sq_key_norm(k_ref):
        kk = k_ref[0]
        ksq = _dot_nt(ones8, kk * kk)
        return jnp.broadcast_to(jnp.max(ksq, axis=-1, keepdims=True), (8, LANES))

    @pl.when(pl.program_id(2) == 0)
    def _():
        kmax_ref[0] = largest_sq_key_norm(ka_ref)
        same_keys = pl.program_id(1) < GQA_PAIRS

        @pl.when(same_keys)
        def _():
            kmax_ref[1] = kmax_ref[0]

        @pl.when(jnp.logical_not(same_keys))
        def _():
            kmax_ref[1] = largest_sq_key_norm(kb_ref)

    qs = (q_ref[0, :, 0:LANES], q_ref[0, :, LANES:2 * LANES])
    bounds = []
    for h in range(2):
        qsq = _dot_nt(ones8, qs[h] * qs[h])[0:1]
        bounds.append(jnp.sqrt(qsq * kmax_ref[h][0:1, 0:1]) * BOUND_SLACK)
    bound_max = jnp.max(jnp.maximum(bounds[0], bounds[1]))

    def finish(accs):
        outs = [a[0:V_DIM] * (1.0 / a[ONE_LANE:ONE_LANE + 1]) for a in accs]
        o_ref[0] = jnp.concatenate(outs, axis=0).T

    def bounded():
        accs = [jnp.zeros((LANES, tq), F32), jnp.zeros((LANES, tq), F32)]
        for c in range(n_chunks):
            for h in range(2):
                ks = k_refs[h][0, c * tk:(c + 1) * tk, :]
                vs = v_refs[h][0, c * tk:(c + 1) * tk, :]
                pt = jnp.exp2(_dot_nt(ks, qs[h]) - bounds[h]).astype(BF16)
                accs[h] = accs[h] + _dot_tn(vs, pt)
        finish(accs)

    def running_max():
        accs = []
        for h in range(2):
            def body(c, carry):
                m, acc = carry
                start = pl.multiple_of(c * tk, tk)
                ks = k_refs[h][0, pl.ds(start, tk), :]
                vs = v_refs[h][0, pl.ds(start, tk), :]
                st = _dot_nt(ks, qs[h])
                m_new = jnp.maximum(m, jnp.max(st, axis=0, keepdims=True))
                pt = jnp.exp2(st - m_new).astype(BF16)
                return m_new, jnp.exp2(m - m_new) * acc + _dot_tn(vs, pt)

            init = (jnp.full((1, tq), -jnp.inf, F32), jnp.zeros((LANES, tq), F32))
            accs.append(lax.fori_loop(0, n_chunks, body, init)[1])
        finish(accs)

    pl.when(bound_max <= SCORE_BOUND_MAX)(bounded)
    pl.when(jnp.logical_not(bound_max <= SCORE_BOUND_MAX))(running_max)


def _attention(q_all, k_all, v_all, tq, tk):
    B, S, _ = q_all.shape
    pairs_per_kv = GQA_GROUP // 2

    def kv_col(p, second):
        return jnp.where(p < GQA_PAIRS, p // pairs_per_kv, GQA_KV_HEADS + 2 * (p - GQA_PAIRS) + second)

    kv_spec = lambda second: pl.BlockSpec((1, S, LANES), lambda b, p, i: (b, 0, kv_col(p, second)))
    return pl.pallas_call(
        functools.partial(_attn_kernel, tk=tk),
        grid=(B, N_PAIRS, S // tq),
        in_specs=[pl.BlockSpec((1, tq, 2 * LANES), lambda b, p, i: (b, i, p)),
                  kv_spec(0), kv_spec(1), kv_spec(0), kv_spec(1)],
        out_specs=pl.BlockSpec((1, tq, LANES), lambda b, p, i: (b, i, p)),
        out_shape=jax.ShapeDtypeStruct((B, S, N_PAIRS * LANES), F32),
        scratch_shapes=[pltpu.VMEM((2, 8, LANES), F32)],
        compiler_params=_cparams(3),
        name="attention",
    )(q_all, k_all, k_all, v_all, v_all)


def _mem_kv_kernel(m_ref, ln_ref, w_ref, o_ref):
    h = (_rms(m_ref[0]) * ln_ref[0]).astype(BF16)
    o_ref[0] = _dot(h, w_ref[0]).astype(BF16)


def _mem_kv(mem, sw, l):
    B, M, D = mem.shape
    ln, w = sw["ln_mem_kv"], sw["w_mem_kv"]
    n = w.shape[2]
    return pl.pallas_call(
        _mem_kv_kernel,
        grid=(B,),
        in_specs=[pl.BlockSpec((1, M, D), lambda b: (b, 0, 0)), _layer_spec(ln, l), _layer_spec(w, l)],
        out_specs=pl.BlockSpec((1, M, n), lambda b: (b, 0, 0)),
        out_shape=jax.ShapeDtypeStruct((B, M, n), BF16),
        compiler_params=_cparams(1),
        name="mem_kv",
    )(mem, ln, w)


def _post_attn_kernel(o_ref, x_ref, on_ref, wo_ref, lnm_ref, wmq_ref, kv_ref, wmo_ref, lnf_ref, wr_ref,
                      x2_ref, h3_ref, aff_ref):
    o = o_ref[0]
    half = o.shape[1] // 2
    merged = (jnp.concatenate([_rms(o[:, :half]), _rms(o[:, half:])], axis=-1) * on_ref[0]).astype(BF16)
    x1 = x_ref[0] + _dot(merged, wo_ref[0])

    h2 = (_rms(x1) * lnm_ref[0]).astype(BF16)
    q = (_dot(h2, wmq_ref[0]) * (MEM_HEAD_DIM ** -0.5 * LOG2E)).astype(BF16)
    kv = kv_ref[0]
    n_mem = MEM_HEADS * MEM_HEAD_DIM
    outs = []
    for hh in range(MEM_HEADS):
        lo, hi = hh * MEM_HEAD_DIM, (hh + 1) * MEM_HEAD_DIM
        s = _dot_nt(q[:, lo:hi], kv[:, lo:hi])
        p = jnp.exp2(s - jnp.max(s, axis=-1, keepdims=True))
        l = jnp.sum(p, axis=-1, keepdims=True)
        outs.append(_dot(p.astype(BF16), kv[:, n_mem + lo:n_mem + hi]) * (1.0 / l))
    oc = jnp.concatenate(outs, axis=-1).astype(BF16)
    x2 = x1 + _dot(oc, wmo_ref[0])
    x2_ref[0] = x2

    h3 = _rms(x2) * lnf_ref[0]
    h3_ref[0] = h3.astype(BF16)
    h_hi = h3.astype(BF16)
    h_lo = (h3 - h_hi.astype(F32)).astype(BF16)
    n_e = wr_ref.shape[1] // 2
    part = _dot_nt(wr_ref[0], h_hi)
    logits = part[:n_e] + part[n_e:] + _dot_nt(wr_ref[0, 0:n_e, :], h_lo)
    e = jnp.exp(logits - jnp.max(logits, axis=0, keepdims=True))
    aff_ref[0] = e * (1.0 / jnp.sum(e, axis=0, keepdims=True))


def _post_attn(o, x, kv_mem, sw, l, ts):
    B, S, D = x.shape
    E = N_EXPERTS
    consts_a = (sw["on"], sw["w_o"], sw["ln_mem"], sw["w_mem_q"])
    consts_b = (sw["w_mem_o"], sw["ln_ffn"], sw["w_router_t"])
    tile = lambda n: pl.BlockSpec((1, ts, n), lambda b, i: (b, i, 0))
    return pl.pallas_call(
        _post_attn_kernel,
        grid=(B, S // ts),
        in_specs=[tile(o.shape[2]), tile(D)] + [_layer_spec(a, l) for a in consts_a]
        + [pl.BlockSpec((1,) + kv_mem.shape[1:], lambda b, i: (b, 0, 0))] + [_layer_spec(a, l) for a in consts_b],
        out_specs=[tile(D), tile(D), pl.BlockSpec((1, E, ts), lambda b, i: (b, 0, i))],
        out_shape=[jax.ShapeDtypeStruct((B, S, D), F32), jax.ShapeDtypeStruct((B, S, D), BF16),
                   jax.ShapeDtypeStruct((B, E, S), F32)],
        compiler_params=_cparams(2),
        name="post_attn",
    )(o, x, *consts_a, kv_mem, *consts_b)


def _select_kernel(aff_ref, slot_ref, pos_ref, *, cap, chunks, seq_rows):
    a = aff_ref[...]
    n_rows = a.shape[0]
    bits = pltpu.bitcast(a, jnp.int32)

    r_i = lax.broadcasted_iota(jnp.int32, (seq_rows, seq_rows), 0)
    c_i = lax.broadcasted_iota(jnp.int32, (seq_rows, seq_rows), 1)
    earlier_chunk = ((r_i // chunks) == (c_i // chunks)) & (c_i < r_i)
    bd_before = jnp.where(earlier_chunk, 1.0, 0.0).astype(BF16)
    l_r = lax.broadcasted_iota(jnp.int32, (LANES, LANES), 0)
    l_c = lax.broadcasted_iota(jnp.int32, (LANES, LANES), 1)
    ones = jnp.ones((LANES, LANES), BF16)
    before = jnp.where(l_r < l_c, 1.0, 0.0).astype(BF16)

    def as01(mask):
        return jnp.where(mask, 1.0, 0.0).astype(BF16)

    def expert_count(x01):
        per_chunk = _dot(x01, ones).reshape(n_rows // chunks, chunks, LANES)
        total = jnp.sum(per_chunk, axis=1, keepdims=True)
        return jnp.broadcast_to(total, per_chunk.shape).reshape(n_rows, LANES)

    def prefix_excl(x01):
        per_chunk = _dot(x01, ones).astype(BF16)
        earlier = [_dot(bd_before, per_chunk[s * seq_rows:(s + 1) * seq_rows]) for s in range(n_rows // seq_rows)]
        return _dot(x01, before) + jnp.concatenate(earlier, axis=0)

    def step(i, theta):
        cand = theta | (jnp.int32(1) << (30 - i))
        cnt = expert_count(as01(bits >= cand))
        return jnp.where(cnt >= cap, cand, theta)

    theta = lax.fori_loop(0, 31, step, jnp.zeros(bits.shape, jnp.int32))
    gt = bits > theta
    eq = bits == theta
    need = cap - expert_count(as01(gt))
    sel = gt | (eq & (prefix_excl(as01(eq)) < need))
    pos = prefix_excl(as01(sel))
    slot_ref[...] = jnp.where(sel, pos, -1.0).astype(jnp.int32)
    pos_ref[...] = pos.astype(jnp.int32)


def _select(aff2, cap, chunks):
    B, seq_rows, _ = aff2.shape
    assert chunks % 8 == 0, "the per-expert reduction reshapes rows into whole sublane tiles"
    n_rows = B * seq_rows
    blk = pl.BlockSpec((n_rows, LANES), lambda i: (0, 0))
    slot, pos = pl.pallas_call(
        functools.partial(_select_kernel, cap=cap, chunks=chunks, seq_rows=seq_rows),
        grid=(1,),
        in_specs=[blk],
        out_specs=[blk, blk],
        out_shape=[jax.ShapeDtypeStruct((n_rows, LANES), jnp.int32)] * 2,
        compiler_params=_cparams(1),
        name="select",
    )(aff2.reshape(n_rows, LANES))
    return slot.reshape(aff2.shape), pos.reshape(aff2.shape)


GATHER_TILE = 2 * LANES
GATHER_WIN = GATHER_TILE + 8
GATHER_WIN_NARROW = LANES
SCATTER_TILE = LANES
SCATTER_WIN = 2 * LANES
SCATTER_WIN_NARROW = 64


def _y_rows(cap):
    return max(-(-cap // LANES) * LANES, SCATTER_WIN)


def _expert_kernel(ps_ref, slot_ref, aff_ref, h_ref, wg_ref, wu_ref, wd_ref, y_ref, xin_ref, gate_ref, *, cap, chunks):
    base = (pl.program_id(0) * N_EXPERTS + pl.program_id(1)) * chunks
    S = h_ref.shape[1]
    n_tiles = S // GATHER_TILE
    bounds = [ps_ref[base + t * (GATHER_TILE // LANES)] for t in range(n_tiles)] + [cap]
    most = bounds[1] - bounds[0]
    for t in range(1, n_tiles):
        most = jnp.maximum(most, bounds[t + 1] - bounds[t])
    xin_ref[...] = jnp.zeros_like(xin_ref)
    gate_ref[...] = jnp.zeros_like(gate_ref)

    def run(win):
        row = lax.broadcasted_iota(jnp.int32, (win, GATHER_TILE), 0)
        for t in range(n_tiles):
            lo, hi = t * GATHER_TILE, (t + 1) * GATHER_TILE
            start = pl.multiple_of((bounds[t] >> 3) << 3, 8)
            hit = (row + start) == slot_ref[0, 0, :, lo:hi]
            onehot = jnp.where(hit, 1.0, 0.0).astype(BF16)
            xin_ref[pl.ds(start, win), :] += _dot(onehot, h_ref[0, lo:hi, :])
            g = jnp.sum(jnp.where(hit, aff_ref[0, 0, :, lo:hi], 0.0), axis=-1, keepdims=True)
            gate_ref[pl.ds(start, win), :] += jnp.broadcast_to(g, (win, LANES))
        x_in = xin_ref[0:cap, :].astype(BF16)
        a = _dot(x_in, wg_ref[0, 0].astype(BF16))
        u = _dot(x_in, wu_ref[0, 0].astype(BF16))
        hm = (a * (1.0 / (1.0 + jnp.exp(-a))) * u).astype(BF16)
        y_ref[0, 0, 0:cap, :] = (_dot(hm, wd_ref[0, 0].astype(BF16)) * gate_ref[0:cap, 0:1]).astype(BF16)
        if y_ref.shape[2] > cap:
            y_ref[0, 0, cap:, :] = jnp.zeros((y_ref.shape[2] - cap, y_ref.shape[3]), BF16)

    narrow = most <= GATHER_WIN_NARROW - 8
    pl.when(narrow)(functools.partial(run, GATHER_WIN_NARROW))
    pl.when(jnp.logical_not(narrow))(functools.partial(run, GATHER_WIN))


def _experts(pstart, slot_row, aff_row, h3, wg, wu, wd, l, cap, chunks):
    B, S, D = h3.shape
    E = N_EXPERTS
    F = wg.shape[3]
    assert S % GATHER_TILE == 0 and cap % 8 == 0
    row = pl.BlockSpec((1, 1, 1, S), lambda b, e, ps: (b, e, 0, 0))
    return pl.pallas_call(
        functools.partial(_expert_kernel, cap=cap, chunks=chunks),
        grid_spec=pltpu.PrefetchScalarGridSpec(
            num_scalar_prefetch=1,
            grid=(B, E),
            in_specs=[row, row, pl.BlockSpec((1, S, D), lambda b, e, ps: (b, 0, 0)),
                      pl.BlockSpec((1, 1, D, F), lambda b, e, ps: (l, e, 0, 0)),
                      pl.BlockSpec((1, 1, D, F), lambda b, e, ps: (l, e, 0, 0)),
                      pl.BlockSpec((1, 1, F, D), lambda b, e, ps: (l, e, 0, 0))],
            out_specs=pl.BlockSpec((1, 1, _y_rows(cap), D), lambda b, e, ps: (b, e, 0, 0)),
            scratch_shapes=[pltpu.VMEM((cap + GATHER_WIN, D), F32), pltpu.VMEM((cap + GATHER_WIN, LANES), F32)]),
        out_shape=jax.ShapeDtypeStruct((B, E, _y_rows(cap), D), BF16),
        compiler_params=_cparams(2),
        name="experts",
    )(pstart, slot_row, aff_row, h3, wg, wu, wd)


def _combine_kernel(ps_ref, x_ref, slot_ref, y_ref, lnf_ref, o_ref, *, cap, chunks, final):
    tc = x_ref.shape[1]
    n_sub = tc // SCATTER_TILE
    n_e = N_EXPERTS
    b, i = pl.program_id(0), pl.program_id(1)
    y_rows = y_ref.shape[2]
    firsts, most = {}, 0
    for sub in range(n_sub):
        chunk = i * n_sub + sub
        for e in range(n_e):
            idx = (b * n_e + e) * chunks + chunk
            firsts[sub, e] = ps_ref[idx]
            most = jnp.maximum(most, jnp.where(chunk == chunks - 1, cap, ps_ref[idx + 1]) - firsts[sub, e])

    def finish(sub, moe):
        lo, hi = sub * SCATTER_TILE, (sub + 1) * SCATTER_TILE
        acc = x_ref[0, lo:hi, :] + moe
        if final:
            acc = _rms(acc) * lnf_ref[...]
        o_ref[0, lo:hi, :] = acc

    def wide():
        col = lax.broadcasted_iota(jnp.int32, (SCATTER_TILE, SCATTER_WIN), 1)
        for sub in range(n_sub):
            slot_t = slot_ref[0, sub * SCATTER_TILE:(sub + 1) * SCATTER_TILE, :]
            hots, wins = [], []
            for e in range(n_e):
                start = pl.multiple_of(jnp.minimum((firsts[sub, e] >> 7) << 7, y_rows - SCATTER_WIN), LANES)
                hots.append(jnp.where(slot_t[:, e:e + 1] == col + start, 1.0, 0.0).astype(BF16))
                wins.append(y_ref[0, e, pl.ds(start, SCATTER_WIN), :])
            finish(sub, _dot(jnp.concatenate(hots, axis=1), jnp.concatenate(wins, axis=0)))

    def narrow():
        w = SCATTER_WIN_NARROW
        lane = lax.broadcasted_iota(jnp.int32, (1, n_e * w), 1)
        offset = (lane & (w - 1)).astype(F32)
        spread = jnp.where(lax.broadcasted_iota(jnp.int32, (n_e, n_e * w), 0) == (lane >> 6), 1.0, 0.0).astype(BF16)
        e_lane = lax.broadcasted_iota(jnp.int32, (1, n_e), 1)
        for sub in range(n_sub):
            slot_t = slot_ref[0, sub * SCATTER_TILE:(sub + 1) * SCATTER_TILE, :]
            start_row = jnp.zeros((1, n_e), jnp.int32)
            wins = []
            for e in range(n_e):
                start = pl.multiple_of(jnp.minimum((firsts[sub, e] >> 4) << 4, y_rows - w), 16)
                start_row = jnp.where(e_lane == e, start, start_row)
                wins.append(y_ref[0, e, pl.ds(start, w), :])
            local = slot_t - start_row
            local = jnp.where((local >= 0) & (local < w), local, -1).astype(F32).astype(BF16)
            onehot = jnp.where(_dot(local, spread) == offset, 1.0, 0.0).astype(BF16)
            finish(sub, _dot(onehot, jnp.concatenate(wins, axis=0)))

    fits = most <= SCATTER_WIN_NARROW - 15
    pl.when(fits)(narrow)
    pl.when(jnp.logical_not(fits))(wide)


def _combine(pstart, x2, slot_t, y, ln_final, cap, chunks, tc, final):
    B, S, D = x2.shape
    E = N_EXPERTS
    assert tc % SCATTER_TILE == 0 and SCATTER_TILE == LANES and SCATTER_WIN_NARROW == 64
    return pl.pallas_call(
        functools.partial(_combine_kernel, cap=cap, chunks=chunks, final=final),
        grid_spec=pltpu.PrefetchScalarGridSpec(
            num_scalar_prefetch=1,
            grid=(B, S // tc),
            in_specs=[pl.BlockSpec((1, tc, D), lambda b, i, ps: (b, i, 0)),
                      pl.BlockSpec((1, tc, E), lambda b, i, ps: (b, i, 0)),
                      pl.BlockSpec((1,) + y.shape[1:], lambda b, i, ps: (b, 0, 0, 0)),
                      pl.BlockSpec(ln_final.shape, lambda b, i, ps: (0, 0))],
            out_specs=pl.BlockSpec((1, tc, D), lambda b, i, ps: (b, i, 0))),
        out_shape=jax.ShapeDtypeStruct((B, S, D), F32),
        compiler_params=_cparams(2),
        name="combine",
    )(pstart, x2, slot_t, y, ln_final)


def _stacked_weights(p):
    w_in = p["w_in"]
    o1 = GQA_HEADS * GQA_HEAD_DIM
    o2 = o1 + GQA_KV_HEADS * GQA_HEAD_DIM
    o3 = o2 + GQA_KV_HEADS * GQA_HEAD_DIM
    o4 = o3 + MLA_Q_RANK
    o5 = o4 + MLA_KV_RANK
    heads = lambda w, n: w.reshape(w.shape[:-1] + (n, w.shape[-1] // n))
    flat = lambda w: w.reshape(w.shape[:-2] + (-1,))
    wq = flat(_lay_gqa(heads(w_in[..., :o1], GQA_HEADS)))
    wk = flat(_lay_gqa(heads(w_in[..., o1:o2], GQA_KV_HEADS)))
    wv = flat(_lay_v(heads(w_in[..., o2:o3], GQA_KV_HEADS)))
    w_kr = w_in[..., o5:]
    wkr = _lay_mla(jnp.zeros(w_kr.shape[:-1] + (MLA_NOPE_DIM,), F32), w_kr)
    blocks = {"q_lat": w_in[..., o3:o4], "kv_lat": w_in[..., o4:o5], "k_rope": wkr, "q_gqa": wq, "k_gqa": wk, "v_gqa": wv}
    wcat = jnp.concatenate([blocks[name] for name, _ in _W_PARTS], axis=-1).astype(BF16)

    wqb = heads(p["w_q_b"], MLA_HEADS)
    wqb = flat(_lay_mla(wqb[..., :MLA_NOPE_DIM], wqb[..., MLA_NOPE_DIM:])).astype(BF16)
    wkvb = heads(p["w_kv_b"], MLA_HEADS)
    k_nope = wkvb[..., :MLA_NOPE_DIM]
    wkb = flat(_lay_mla(k_nope, jnp.zeros(k_nope.shape[:-1] + (MLA_ROPE_DIM,), F32))).astype(BF16)
    wvb = flat(_lay_v(wkvb[..., MLA_NOPE_DIM:])).astype(BF16)

    q_scale = GQA_HEAD_DIM ** -0.5 * LOG2E
    row = lambda v: v[:, None, :]
    return {
        "ln_mix": row(p["ln_mix"]), "wcat": wcat, "wqb": wqb, "wkb": wkb, "wvb": wvb,
        "gq": row(_lay_gqa(p["gqa_q_norm"] * q_scale)), "gk": row(_lay_gqa(p["gqa_k_norm"])),
        "gql": row(p["mla_q_norm"]), "gkv": row(p["mla_kv_norm"]),
        "on": row(jnp.concatenate([p["out_norm_gqa"], p["out_norm_mla"]], axis=-1)),
        "w_o": p["w_o"].astype(BF16), "ln_mem": row(p["ln_mem"]), "ln_mem_kv": row(p["ln_mem_kv"]),
        "w_mem_q": p["w_mem_q"].astype(BF16), "w_mem_kv": p["w_mem_kv"].astype(BF16),
        "w_mem_o": p["w_mem_o"].astype(BF16), "ln_ffn": row(p["ln_ffn"]),
        "w_router_t": _split_hi_lo(jnp.swapaxes(p["w_router"], 1, 2)),
    }


def _split_hi_lo(w):
    hi = w.astype(BF16)
    lo = (w - hi.astype(F32)).astype(BF16)
    return jnp.concatenate([hi, lo], axis=1)


def _tables(seq_len):
    cos_g, sin_g, cos_m, sin_m = _rope_tables(seq_len)
    mq_scale = MLA_QK_DIM ** -0.5 * LOG2E
    one_g = jnp.zeros((GQA_KV_HEADS, LANES), F32).at[:, ONE_LANE].set(1.0)
    one_m = jnp.zeros((MLA_HEADS, LANES), F32).at[:, ONE_LANE].set(1.0)
    return {"cos_g": cos_g, "sin_g": sin_g, "cos_m": cos_m, "sin_m": sin_m,
            "cos_mq": cos_m * mq_scale, "sin_mq": sin_m * mq_scale,
            "one_g": one_g.reshape(1, -1), "one_m": one_m.reshape(1, -1)}


def _pick(n, pref):
    t = min(n, pref)
    assert n % t == 0, (n, t)
    return t


def kernel(x, mem, ln_mix, w_in, gqa_q_norm, gqa_k_norm, mla_q_norm, mla_kv_norm, w_q_b, w_kv_b, out_norm_gqa,
           out_norm_mla, w_o, ln_mem, ln_mem_kv, w_mem_q, w_mem_kv, w_mem_o, ln_ffn, w_router, w_gate, w_up,
           w_down, ln_final):
    p = dict(ln_mix=ln_mix, w_in=w_in, gqa_q_norm=gqa_q_norm, gqa_k_norm=gqa_k_norm, mla_q_norm=mla_q_norm,
             mla_kv_norm=mla_kv_norm, w_q_b=w_q_b, w_kv_b=w_kv_b, out_norm_gqa=out_norm_gqa,
             out_norm_mla=out_norm_mla, w_o=w_o, ln_mem=ln_mem, ln_mem_kv=ln_mem_kv, w_mem_q=w_mem_q,
             w_mem_kv=w_mem_kv, w_mem_o=w_mem_o, ln_ffn=ln_ffn, w_router=w_router, w_gate=w_gate, w_up=w_up,
             w_down=w_down)
    B, S, D = x.shape
    depth = w_in.shape[0]
    E = N_EXPERTS
    assert S % LANES == 0 and S % GRID_W == 0
    cap = EC_CAPACITY_FACTOR * S // E
    chunks = S // LANES
    tabs = _tables(S)
    ln_final2 = ln_final.reshape(1, -1)
    ts_in, ts_post, tq, tk, tc = _pick(S, 1024), _pick(S, 1024), _pick(S, 4096), _pick(S, 512), _pick(S, 512)
    sw = _stacked_weights(p)

    for l in range(depth):
        q_all, k_all, v_all = _mixer_in(x, sw, l, tabs, ts_in)
        o = _attention(q_all, k_all, v_all, tq, tk)
        kv_mem = _mem_kv(mem, sw, l)
        x2, h3, aff = _post_attn(o, x, kv_mem, sw, l, ts_post)
        slot, pos = _select(aff.reshape(B, E * chunks, LANES), cap, chunks)
        slot = slot.reshape(B, E, S)
        pstart = jnp.pad(pos[:, :, 0].reshape(B * E * chunks), (0, 1))
        y = _experts(pstart, slot.reshape(B, E, 1, S), aff.reshape(B, E, 1, S), h3, w_gate, w_up, w_down, l,
                     cap, chunks)
        x = _combine(pstart, x2, jnp.swapaxes(slot, 1, 2), y, ln_final2, cap, chunks, tc, final=(l == depth - 1))
    return x
```

```python
import functools
import math

import numpy as np
import jax
import jax.numpy as jnp
from jax import lax
from jax.experimental import pallas as pl
from jax.experimental.pallas import tpu as pltpu

F32 = jnp.float32
BF16 = jnp.bfloat16

GRID_W = 64
ROPE_THETA = 10000.0
EPS = 1e-6
GQA_HEADS = 8
GQA_KV_HEADS = 2
GQA_GROUP = GQA_HEADS // GQA_KV_HEADS
GQA_HEAD_DIM = 64
MLA_HEADS = 8
MLA_Q_RANK = 256
MLA_KV_RANK = 128
MLA_NOPE_DIM = 64
MLA_ROPE_DIM = 32
MLA_V_DIM = 64
MLA_QK_DIM = MLA_NOPE_DIM + MLA_ROPE_DIM
MEM_HEADS = 4
MEM_HEAD_DIM = 128
N_EXPERTS = 16
EC_CAPACITY_FACTOR = 2

LANES = 128
LOG2E = math.log2(math.e)
VMEM_LIMIT = 56 * 1024 * 1024

N_HEADS = GQA_HEADS + MLA_HEADS
N_PAIRS = N_HEADS // 2
GQA_PAIRS = GQA_HEADS // 2


def _cparams(n_axes):
    return pltpu.CompilerParams(dimension_semantics=("arbitrary",) * n_axes, vmem_limit_bytes=VMEM_LIMIT)


def _rms(x, eps=EPS):
    return x * lax.rsqrt(jnp.mean(x * x, axis=-1, keepdims=True) + eps)


def _dot(a, b):
    return jnp.dot(a, b, preferred_element_type=F32)


def _dot_nt(a, b):
    return lax.dot_general(a, b, (((1,), (1,)), ((), ())), preferred_element_type=F32)


V_DIM = 64
ONE_LANE = V_DIM


def _zeros_like_cols(w, n):
    return jnp.zeros(w.shape[:-1] + (n,), w.dtype)


def _lay_gqa(w):
    z = _zeros_like_cols(w, 32)
    return jnp.concatenate([w[..., :32], z, w[..., 32:], z], axis=-1)


def _lay_mla(nope, rope):
    z = _zeros_like_cols(nope, 16)
    return jnp.concatenate([nope[..., :32], rope[..., :16], z, nope[..., 32:], rope[..., 16:], z], axis=-1)


def _lay_v(v):
    return jnp.concatenate([v, _zeros_like_cols(v, LANES - V_DIM)], axis=-1)


def _rope_tables(seq_len):
    rows = seq_len // GRID_W
    row = jnp.repeat(jnp.arange(rows, dtype=F32), GRID_W)
    col = jnp.tile(jnp.arange(GRID_W, dtype=F32), rows)

    def angles(rot_dim):
        axis_dim = rot_dim // 2
        inv_freq = ROPE_THETA ** (-jnp.arange(0, axis_dim, 2, dtype=F32) / axis_dim)
        ang = jnp.concatenate([row[:, None] * inv_freq[None, :], col[:, None] * inv_freq[None, :]], axis=-1)
        return jnp.cos(ang), jnp.sin(ang)

    cg, sg = angles(GQA_HEAD_DIM)
    cm, sm = angles(MLA_ROPE_DIM)
    cos_g = _lay_gqa(jnp.concatenate([cg, cg], axis=-1))
    sin_g = _lay_gqa(jnp.concatenate([-sg, sg], axis=-1))
    one = jnp.ones((seq_len, MLA_NOPE_DIM), F32)
    cos_m = _lay_mla(one, jnp.concatenate([cm, cm], axis=-1))
    sin_m = _lay_mla(0.0 * one, jnp.concatenate([-sm, sm], axis=-1))
    return cos_g, sin_g, cos_m, sin_m


_W_PARTS = (("q_gqa", GQA_HEADS * LANES), ("k_gqa", GQA_KV_HEADS * LANES), ("v_gqa", GQA_KV_HEADS * LANES),
            ("q_lat", MLA_Q_RANK), ("kv_lat", MLA_KV_RANK), ("k_rope", LANES))
_W_STARTS = dict(zip((n for n, _ in _W_PARTS), (int(v) for v in np.cumsum((0,) + tuple(w for _, w in _W_PARTS)))))
_W_WIDTH = dict(_W_PARTS)


def _mixer_in_kernel(x_ref, ln_ref, wcat_ref, wqb_ref, wkb_ref, wvb_ref, gq_ref, gk_ref, gql_ref, gkv_ref,
                     cg_ref, sg_ref, cmq_ref, smq_ref, cmk_ref, smk_ref, oneg_ref, onem_ref,
                     q_ref, k_ref, v_ref):
    n_vg = GQA_KV_HEADS * LANES
    h = (_rms(x_ref[0]) * ln_ref[0]).astype(BF16)
    proj = _dot(h, wcat_ref[0])

    def part(name, j=0, width=None):
        lo = _W_STARTS[name] + j * LANES
        return proj[:, lo:lo + (width or _W_WIDTH[name])]

    cg, sg = cg_ref[...], sg_ref[...]

    def head_norm_rope(blk, gain):
        ss = jnp.sum(blk * blk, axis=-1, keepdims=True) * (1.0 / GQA_HEAD_DIM)
        y = blk * lax.rsqrt(ss + EPS) * gain
        return y * cg + pltpu.roll(y, 64, 1) * sg

    for j in range(GQA_HEADS):
        q_ref[0, :, j * LANES:(j + 1) * LANES] = head_norm_rope(part("q_gqa", j, LANES), gq_ref[0]).astype(BF16)
    for j in range(GQA_KV_HEADS):
        k_ref[0, :, j * LANES:(j + 1) * LANES] = head_norm_rope(part("k_gqa", j, LANES), gk_ref[0]).astype(BF16)
    v_ref[0, :, 0:n_vg] = (part("v_gqa") + oneg_ref[...]).astype(BF16)

    c_q = (_rms(part("q_lat")) * gql_ref[0]).astype(BF16)
    qm = _dot(c_q, wqb_ref[0])
    cmq, smq = cmq_ref[...], smq_ref[...]
    for j in range(MLA_HEADS):
        blk = qm[:, j * LANES:(j + 1) * LANES]
        q_ref[0, :, (GQA_HEADS + j) * LANES:(GQA_HEADS + j + 1) * LANES] = (
            blk * cmq + pltpu.roll(blk, 64, 1) * smq).astype(BF16)

    c_kv = (_rms(part("kv_lat")) * gkv_ref[0]).astype(BF16)
    kn = _dot(c_kv, wkb_ref[0])
    vm = _dot(c_kv, wvb_ref[0]) + onem_ref[...]
    kr = part("k_rope")
    kr = kr * cmk_ref[...] + pltpu.roll(kr, 64, 1) * smk_ref[...]
    for j in range(MLA_HEADS):
        k_ref[0, :, (GQA_KV_HEADS + j) * LANES:(GQA_KV_HEADS + j + 1) * LANES] = (
            kn[:, j * LANES:(j + 1) * LANES] + kr).astype(BF16)
    v_ref[0, :, n_vg:] = vm.astype(BF16)


def _layer_spec(a, l):
    zeros = (0,) * (a.ndim - 1)
    return pl.BlockSpec((1,) + a.shape[1:], lambda *grid_ids: (l,) + zeros)


def _mixer_in(x, sw, l, tabs, ts):
    B, S, D = x.shape
    nq, nk = N_HEADS * LANES, (GQA_KV_HEADS + MLA_HEADS) * LANES
    nv = nk
    full = lambda a: pl.BlockSpec(a.shape, lambda b, i: (0,) * a.ndim)
    tab = pl.BlockSpec((ts, LANES), lambda b, i: (i, 0))
    consts = (sw["ln_mix"], sw["wcat"], sw["wqb"], sw["wkb"], sw["wvb"], sw["gq"], sw["gk"], sw["gql"], sw["gkv"])
    return pl.pallas_call(
        _mixer_in_kernel,
        grid=(B, S // ts),
        in_specs=[pl.BlockSpec((1, ts, D), lambda b, i: (b, i, 0))] + [_layer_spec(a, l) for a in consts]
        + [tab] * 6 + [full(tabs["one_g"]), full(tabs["one_m"])],
        out_specs=[pl.BlockSpec((1, ts, nq), lambda b, i: (b, i, 0)),
                   pl.BlockSpec((1, ts, nk), lambda b, i: (b, i, 0)),
                   pl.BlockSpec((1, ts, nv), lambda b, i: (b, i, 0))],
        out_shape=[jax.ShapeDtypeStruct((B, S, nq), BF16), jax.ShapeDtypeStruct((B, S, nk), BF16),
                   jax.ShapeDtypeStruct((B, S, nv), BF16)],
        compiler_params=_cparams(2),
        name="mixer_in",
    )(x, *consts, tabs["cos_g"], tabs["sin_g"], tabs["cos_mq"], tabs["sin_mq"], tabs["cos_m"], tabs["sin_m"],
      tabs["one_g"], tabs["one_m"])


SCORE_BOUND_MAX = 40.0
BOUND_SLACK = 1.02


def _dot_tn(a, b):
    return lax.dot_general(a, b, (((0,), (0,)), ((), ())), preferred_element_type=F32)


def _attn_kernel(q_ref, ka_ref, kb_ref, va_ref, vb_ref, o_ref, kmax_ref, *, tk):
    tq = q_ref.shape[1]
    n_chunks = ka_ref.shape[1] // tk
    ones8 = jnp.ones((8, LANES), BF16)
    k_refs = (ka_ref, kb_ref)
    v_refs = (va_ref, vb_ref)

    @pl.when(pl.program_id(2) == 0)
    def _():
        for h in range(2):
            kk = k_refs[h][0]
            ksq = _dot_nt(ones8, kk * kk)
            kmax_ref[h] = jnp.broadcast_to(jnp.max(ksq, axis=-1, keepdims=True), (8, LANES))

    qs = (q_ref[0, :, 0:LANES], q_ref[0, :, LANES:2 * LANES])
    bounds = []
    for h in range(2):
        qsq = _dot_nt(ones8, qs[h] * qs[h])[0:1]
        bounds.append(jnp.sqrt(qsq * kmax_ref[h][0:1, 0:1]) * BOUND_SLACK)
    bound_max = jnp.max(jnp.maximum(bounds[0], bounds[1]))

    def finish(accs):
        outs = [a[0:V_DIM] * (1.0 / a[ONE_LANE:ONE_LANE + 1]) for a in accs]
        o_ref[0] = jnp.concatenate(outs, axis=0).T

    def bounded():
        accs = [jnp.zeros((LANES, tq), F32), jnp.zeros((LANES, tq), F32)]
        for c in range(n_chunks):
            for h in range(2):
                ks = k_refs[h][0, c * tk:(c + 1) * tk, :]
                vs = v_refs[h][0, c * tk:(c + 1) * tk, :]
                pt = jnp.exp2(_dot_nt(ks, qs[h]) - bounds[h]).astype(BF16)
                accs[h] = accs[h] + _dot_tn(vs, pt)
        finish(accs)

    def running_max():
        accs = []
        for h in range(2):
            def body(c, carry):
                m, acc = carry
                start = pl.multiple_of(c * tk, tk)
                ks = k_refs[h][0, pl.ds(start, tk), :]
                vs = v_refs[h][0, pl.ds(start, tk), :]
                st = _dot_nt(ks, qs[h])
                m_new = jnp.maximum(m, jnp.max(st, axis=0, keepdims=True))
                pt = jnp.exp2(st - m_new).astype(BF16)
                return m_new, jnp.exp2(m - m_new) * acc + _dot_tn(vs, pt)

            init = (jnp.full((1, tq), -jnp.inf, F32), jnp.zeros((LANES, tq), F32))
            accs.append(lax.fori_loop(0, n_chunks, body, init)[1])
        finish(accs)

    bounded()
    pl.when(jnp.logical_not(bound_max <= SCORE_BOUND_MAX))(running_max)


def _attention(q_all, k_all, v_all, tq, tk):
    B, S, _ = q_all.shape
    pairs_per_kv = GQA_GROUP // 2

    def kv_col(p, second):
        return jnp.where(p < GQA_PAIRS, p // pairs_per_kv, GQA_KV_HEADS + 2 * (p - GQA_PAIRS) + second)

    kv_spec = lambda second: pl.BlockSpec((1, S, LANES), lambda b, p, i: (b, 0, kv_col(p, second)))
    return pl.pallas_call(
        functools.partial(_attn_kernel, tk=tk),
        grid=(B, N_PAIRS, S // tq),
        in_specs=[pl.BlockSpec((1, tq, 2 * LANES), lambda b, p, i: (b, i, p)),
                  kv_spec(0), kv_spec(1), kv_spec(0), kv_spec(1)],
        out_specs=pl.BlockSpec((1, tq, LANES), lambda b, p, i: (b, i, p)),
        out_shape=jax.ShapeDtypeStruct((B, S, N_PAIRS * LANES), F32),
        scratch_shapes=[pltpu.VMEM((2, 8, LANES), F32)],
        compiler_params=_cparams(3),
        name="attention",
    )(q_all, k_all, k_all, v_all, v_all)


def _mem_kv_kernel(m_ref, ln_ref, w_ref, o_ref):
    h = (_rms(m_ref[0]) * ln_ref[0]).astype(BF16)
    o_ref[0, 0] = _dot(h, w_ref[0]).astype(BF16)


def _mem_kv(mem, sw):
    B, M, D = mem.shape
    ln, w = sw["ln_mem_kv"], sw["w_mem_kv"]
    L, _, n = w.shape
    return pl.pallas_call(
        _mem_kv_kernel,
        grid=(L, B),
        in_specs=[pl.BlockSpec((1, M, D), lambda l, b: (b, 0, 0)),
                  pl.BlockSpec((1, 1, D), lambda l, b: (l, 0, 0)),
                  pl.BlockSpec((1, D, n), lambda l, b: (l, 0, 0))],
        out_specs=pl.BlockSpec((1, 1, M, n), lambda l, b: (l, b, 0, 0)),
        out_shape=jax.ShapeDtypeStruct((L, B, M, n), BF16),
        compiler_params=_cparams(2),
        name="mem_kv",
    )(mem, ln, w)


def _post_attn_kernel(o_ref, x_ref, on_ref, wo_ref, lnm_ref, wmq_ref, kv_ref, wmo_ref, lnf_ref, wr_ref,
                      x2_ref, h3_ref, aff_ref):
    o = o_ref[0]
    half = o.shape[1] // 2
    merged = (jnp.concatenate([_rms(o[:, :half]), _rms(o[:, half:])], axis=-1) * on_ref[0]).astype(BF16)
    x1 = x_ref[0] + _dot(merged, wo_ref[0])

    h2 = (_rms(x1) * lnm_ref[0]).astype(BF16)
    q = (_dot(h2, wmq_ref[0]) * (MEM_HEAD_DIM ** -0.5 * LOG2E)).astype(BF16)
    kv = kv_ref[0, 0]
    n_mem = MEM_HEADS * MEM_HEAD_DIM
    outs = []
    for hh in range(MEM_HEADS):
        lo, hi = hh * MEM_HEAD_DIM, (hh + 1) * MEM_HEAD_DIM
        s = _dot_nt(q[:, lo:hi], kv[:, lo:hi])
        p = jnp.exp2(s - jnp.max(s, axis=-1, keepdims=True))
        l = jnp.sum(p, axis=-1, keepdims=True)
        outs.append(_dot(p.astype(BF16), kv[:, n_mem + lo:n_mem + hi]) * (1.0 / l))
    oc = jnp.concatenate(outs, axis=-1).astype(BF16)
    x2 = x1 + _dot(oc, wmo_ref[0])
    x2_ref[0] = x2

    h3 = _rms(x2) * lnf_ref[0]
    h3_ref[0] = h3.astype(BF16)
    h_hi = h3.astype(BF16)
    h_lo = (h3 - h_hi.astype(F32)).astype(BF16)
    n_e = wr_ref.shape[1] // 2
    part = _dot_nt(wr_ref[0], h_hi)
    logits = part[:n_e] + part[n_e:] + _dot_nt(wr_ref[0, 0:n_e, :], h_lo)
    e = jnp.exp(logits - jnp.max(logits, axis=0, keepdims=True))
    aff_ref[0] = e * (1.0 / jnp.sum(e, axis=0, keepdims=True))


def _post_attn(o, x, kv_mem, sw, l, ts):
    B, S, D = x.shape
    E = N_EXPERTS
    consts_a = (sw["on"], sw["w_o"], sw["ln_mem"], sw["w_mem_q"])
    consts_b = (sw["w_mem_o"], sw["ln_ffn"], sw["w_router_t"])
    tile = lambda n: pl.BlockSpec((1, ts, n), lambda b, i: (b, i, 0))
    return pl.pallas_call(
        _post_attn_kernel,
        grid=(B, S // ts),
        in_specs=[tile(o.shape[2]), tile(D)] + [_layer_spec(a, l) for a in consts_a]
        + [pl.BlockSpec((1, 1) + kv_mem.shape[2:], lambda b, i: (l, b, 0, 0))] + [_layer_spec(a, l) for a in consts_b],
        out_specs=[tile(D), tile(D), pl.BlockSpec((1, E, ts), lambda b, i: (b, 0, i))],
        out_shape=[jax.ShapeDtypeStruct((B, S, D), F32), jax.ShapeDtypeStruct((B, S, D), BF16),
                   jax.ShapeDtypeStruct((B, E, S), F32)],
        compiler_params=_cparams(2),
        name="post_attn",
    )(o, x, *consts_a, kv_mem, *consts_b)


def _select_kernel(aff_ref, slot_ref, pos_ref, *, cap, chunks, seq_rows):
    a = aff_ref[...]
    n_rows = a.shape[0]
    bits = pltpu.bitcast(a, jnp.int32)

    r_i = lax.broadcasted_iota(jnp.int32, (seq_rows, seq_rows), 0)
    c_i = lax.broadcasted_iota(jnp.int32, (seq_rows, seq_rows), 1)
    earlier_chunk = ((r_i // chunks) == (c_i // chunks)) & (c_i < r_i)
    bd_before = jnp.where(earlier_chunk, 1.0, 0.0).astype(BF16)
    l_r = lax.broadcasted_iota(jnp.int32, (LANES, LANES), 0)
    l_c = lax.broadcasted_iota(jnp.int32, (LANES, LANES), 1)
    ones = jnp.ones((LANES, LANES), BF16)
    before = jnp.where(l_r < l_c, 1.0, 0.0).astype(BF16)

    def as01(mask):
        return jnp.where(mask, 1.0, 0.0).astype(BF16)

    def expert_count(x01):
        per_chunk = _dot(x01, ones).reshape(n_rows // chunks, chunks, LANES)
        total = jnp.sum(per_chunk, axis=1, keepdims=True)
        return jnp.broadcast_to(total, per_chunk.shape).reshape(n_rows, LANES)

    def prefix_excl(x01):
        per_chunk = _dot(x01, ones).astype(BF16)
        earlier = [_dot(bd_before, per_chunk[s * seq_rows:(s + 1) * seq_rows]) for s in range(n_rows // seq_rows)]
        return _dot(x01, before) + jnp.concatenate(earlier, axis=0)

    def step(i, theta):
        cand = theta | (jnp.int32(1) << (30 - i))
        cnt = expert_count(as01(bits >= cand))
        return jnp.where(cnt >= cap, cand, theta)

    theta = lax.fori_loop(0, 31, step, jnp.zeros(bits.shape, jnp.int32))
    gt = bits > theta
    eq = bits == theta
    need = cap - expert_count(as01(gt))
    sel = gt | (eq & (prefix_excl(as01(eq)) < need))
    pos = prefix_excl(as01(sel))
    slot_ref[...] = jnp.where(sel, pos, -1.0).astype(jnp.int32)
    pos_ref[...] = pos.astype(jnp.int32)


def _select(aff2, cap, chunks):
    B, seq_rows, _ = aff2.shape
    assert chunks % 8 == 0, "the per-expert reduction reshapes rows into whole sublane tiles"
    n_rows = B * seq_rows
    blk = pl.BlockSpec((n_rows, LANES), lambda i: (0, 0))
    slot, pos = pl.pallas_call(
        functools.partial(_select_kernel, cap=cap, chunks=chunks, seq_rows=seq_rows),
        grid=(1,),
        in_specs=[blk],
        out_specs=[blk, blk],
        out_shape=[jax.ShapeDtypeStruct((n_rows, LANES), jnp.int32)] * 2,
        compiler_params=_cparams(1),
        name="select",
    )(aff2.reshape(n_rows, LANES))
    return slot.reshape(aff2.shape), pos.reshape(aff2.shape)


GATHER_TILE = 2 * LANES
GATHER_WIN = GATHER_TILE + 8
GATHER_WIN_NARROW = LANES
SCATTER_TILE = LANES
SCATTER_WIN = 2 * LANES
SCATTER_WIN_NARROW = 64


def _y_rows(cap):
    return max(-(-cap // LANES) * LANES, SCATTER_WIN)


def _expert_kernel(ps_ref, slot_ref, aff_ref, h_ref, wg_ref, wu_ref, wd_ref, y_ref, xin_ref, gate_ref, *, cap, chunks):
    base = (pl.program_id(0) * N_EXPERTS + pl.program_id(1)) * chunks
    S = h_ref.shape[1]
    n_tiles = S // GATHER_TILE
    bounds = [ps_ref[base + t * (GATHER_TILE // LANES)] for t in range(n_tiles)] + [cap]
    most = bounds[1] - bounds[0]
    for t in range(1, n_tiles):
        most = jnp.maximum(most, bounds[t + 1] - bounds[t])
    xin_ref[...] = jnp.zeros_like(xin_ref)
    gate_ref[...] = jnp.zeros_like(gate_ref)

    def run(win):
        row = lax.broadcasted_iota(jnp.int32, (win, GATHER_TILE), 0)
        for t in range(n_tiles):
            lo, hi = t * GATHER_TILE, (t + 1) * GATHER_TILE
            start = pl.multiple_of((bounds[t] >> 3) << 3, 8)
            hit = (row + start) == slot_ref[0, 0, :, lo:hi]
            onehot = jnp.where(hit, 1.0, 0.0).astype(BF16)
            xin_ref[pl.ds(start, win), :] += _dot(onehot, h_ref[0, lo:hi, :])
            g = jnp.sum(jnp.where(hit, aff_ref[0, 0, :, lo:hi], 0.0), axis=-1, keepdims=True)
            gate_ref[pl.ds(start, win), :] += jnp.broadcast_to(g, (win, LANES))
        x_in = xin_ref[0:cap, :].astype(BF16)
        a = _dot(x_in, wg_ref[0, 0].astype(BF16))
        u = _dot(x_in, wu_ref[0, 0].astype(BF16))
        hm = (a * (1.0 / (1.0 + jnp.exp(-a))) * u).astype(BF16)
        y_ref[0, 0, 0:cap, :] = (_dot(hm, wd_ref[0, 0].astype(BF16)) * gate_ref[0:cap, 0:1]).astype(BF16)
        if y_ref.shape[2] > cap:
            y_ref[0, 0, cap:, :] = jnp.zeros((y_ref.shape[2] - cap, y_ref.shape[3]), BF16)

    narrow = most <= GATHER_WIN_NARROW - 8
    pl.when(narrow)(functools.partial(run, GATHER_WIN_NARROW))
    pl.when(jnp.logical_not(narrow))(functools.partial(run, GATHER_WIN))


def _experts(pstart, slot_row, aff_row, h3, wg, wu, wd, l, cap, chunks):
    B, S, D = h3.shape
    E = N_EXPERTS
    F = wg.shape[3]
    assert S % GATHER_TILE == 0 and cap % 8 == 0
    row = pl.BlockSpec((1, 1, 1, S), lambda b, e, ps: (b, e, 0, 0))
    return pl.pallas_call(
        functools.partial(_expert_kernel, cap=cap, chunks=chunks),
        grid_spec=pltpu.PrefetchScalarGridSpec(
            num_scalar_prefetch=1,
            grid=(B, E),
            in_specs=[row, row, pl.BlockSpec((1, S, D), lambda b, e, ps: (b, 0, 0)),
                      pl.BlockSpec((1, 1, D, F), lambda b, e, ps: (l, e, 0, 0)),
                      pl.BlockSpec((1, 1, D, F), lambda b, e, ps: (l, e, 0, 0)),
                      pl.BlockSpec((1, 1, F, D), lambda b, e, ps: (l, e, 0, 0))],
            out_specs=pl.BlockSpec((1, 1, _y_rows(cap), D), lambda b, e, ps: (b, e, 0, 0)),
            scratch_shapes=[pltpu.VMEM((cap + GATHER_WIN, D), F32), pltpu.VMEM((cap + GATHER_WIN, LANES), F32)]),
        out_shape=jax.ShapeDtypeStruct((B, E, _y_rows(cap), D), BF16),
        compiler_params=_cparams(2),
        name="experts",
    )(pstart, slot_row, aff_row, h3, wg, wu, wd)


def _combine_kernel(ps_ref, x_ref, slot_ref, y_ref, lnf_ref, o_ref, *, cap, chunks, final):
    tc = x_ref.shape[1]
    n_sub = tc // SCATTER_TILE
    n_e = N_EXPERTS
    b, i = pl.program_id(0), pl.program_id(1)
    y_rows = y_ref.shape[2]
    firsts, most = {}, 0
    for sub in range(n_sub):
        chunk = i * n_sub + sub
        for e in range(n_e):
            idx = (b * n_e + e) * chunks + chunk
            firsts[sub, e] = ps_ref[idx]
            most = jnp.maximum(most, jnp.where(chunk == chunks - 1, cap, ps_ref[idx + 1]) - firsts[sub, e])

    def finish(sub, moe):
        lo, hi = sub * SCATTER_TILE, (sub + 1) * SCATTER_TILE
        acc = x_ref[0, lo:hi, :] + moe
        if final:
            acc = _rms(acc) * lnf_ref[...]
        o_ref[0, lo:hi, :] = acc

    def wide():
        col = lax.broadcasted_iota(jnp.int32, (SCATTER_TILE, SCATTER_WIN), 1)
        for sub in range(n_sub):
            slot_t = slot_ref[0, sub * SCATTER_TILE:(sub + 1) * SCATTER_TILE, :]
            hots, wins = [], []
            for e in range(n_e):
                start = pl.multiple_of(jnp.minimum((firsts[sub, e] >> 7) << 7, y_rows - SCATTER_WIN), LANES)
                hots.append(jnp.where(slot_t[:, e:e + 1] == col + start, 1.0, 0.0).astype(BF16))
                wins.append(y_ref[0, e, pl.ds(start, SCATTER_WIN), :])
            finish(sub, _dot(jnp.concatenate(hots, axis=1), jnp.concatenate(wins, axis=0)))

    def narrow():
        w = SCATTER_WIN_NARROW
        lane = lax.broadcasted_iota(jnp.int32, (1, n_e * w), 1)
        offset = (lane & (w - 1)).astype(F32)
        spread = jnp.where(lax.broadcasted_iota(jnp.int32, (n_e, n_e * w), 0) == (lane >> 6), 1.0, 0.0).astype(BF16)
        e_lane = lax.broadcasted_iota(jnp.int32, (1, n_e), 1)
        for sub in range(n_sub):
            slot_t = slot_ref[0, sub * SCATTER_TILE:(sub + 1) * SCATTER_TILE, :]
            start_row = jnp.zeros((1, n_e), jnp.int32)
            wins = []
            for e in range(n_e):
                start = pl.multiple_of(jnp.minimum((firsts[sub, e] >> 4) << 4, y_rows - w), 16)
                start_row = jnp.where(e_lane == e, start, start_row)
                wins.append(y_ref[0, e, pl.ds(start, w), :])
            local = slot_t - start_row
            local = jnp.where((local >= 0) & (local < w), local, -1).astype(F32).astype(BF16)
            onehot = jnp.where(_dot(local, spread) == offset, 1.0, 0.0).astype(BF16)
            finish(sub, _dot(onehot, jnp.concatenate(wins, axis=0)))

    fits = most <= SCATTER_WIN_NARROW - 15
    pl.when(fits)(narrow)
    pl.when(jnp.logical_not(fits))(wide)


def _combine(pstart, x2, slot_t, y, ln_final, cap, chunks, tc, final):
    B, S, D = x2.shape
    E = N_EXPERTS
    assert tc % SCATTER_TILE == 0 and SCATTER_TILE == LANES and SCATTER_WIN_NARROW == 64
    return pl.pallas_call(
        functools.partial(_combine_kernel, cap=cap, chunks=chunks, final=final),
        grid_spec=pltpu.PrefetchScalarGridSpec(
            num_scalar_prefetch=1,
            grid=(B, S // tc),
            in_specs=[pl.BlockSpec((1, tc, D), lambda b, i, ps: (b, i, 0)),
                      pl.BlockSpec((1, tc, E), lambda b, i, ps: (b, i, 0)),
                      pl.BlockSpec((1,) + y.shape[1:], lambda b, i, ps: (b, 0, 0, 0)),
                      pl.BlockSpec(ln_final.shape, lambda b, i, ps: (0, 0))],
            out_specs=pl.BlockSpec((1, tc, D), lambda b, i, ps: (b, i, 0))),
        out_shape=jax.ShapeDtypeStruct((B, S, D), F32),
        compiler_params=_cparams(2),
        name="combine",
    )(pstart, x2, slot_t, y, ln_final)


def _stacked_weights(p):
    w_in = p["w_in"]
    o1 = GQA_HEADS * GQA_HEAD_DIM
    o2 = o1 + GQA_KV_HEADS * GQA_HEAD_DIM
    o3 = o2 + GQA_KV_HEADS * GQA_HEAD_DIM
    o4 = o3 + MLA_Q_RANK
    o5 = o4 + MLA_KV_RANK
    heads = lambda w, n: w.reshape(w.shape[:-1] + (n, w.shape[-1] // n))
    flat = lambda w: w.reshape(w.shape[:-2] + (-1,))
    wq = flat(_lay_gqa(heads(w_in[..., :o1], GQA_HEADS)))
    wk = flat(_lay_gqa(heads(w_in[..., o1:o2], GQA_KV_HEADS)))
    wv = flat(_lay_v(heads(w_in[..., o2:o3], GQA_KV_HEADS)))
    w_kr = w_in[..., o5:]
    wkr = _lay_mla(jnp.zeros(w_kr.shape[:-1] + (MLA_NOPE_DIM,), F32), w_kr)
    blocks = {"q_lat": w_in[..., o3:o4], "kv_lat": w_in[..., o4:o5], "k_rope": wkr, "q_gqa": wq, "k_gqa": wk, "v_gqa": wv}
    wcat = jnp.concatenate([blocks[name] for name, _ in _W_PARTS], axis=-1).astype(BF16)

    wqb = heads(p["w_q_b"], MLA_HEADS)
    wqb = flat(_lay_mla(wqb[..., :MLA_NOPE_DIM], wqb[..., MLA_NOPE_DIM:])).astype(BF16)
    wkvb = heads(p["w_kv_b"], MLA_HEADS)
    k_nope = wkvb[..., :MLA_NOPE_DIM]
    wkb = flat(_lay_mla(k_nope, jnp.zeros(k_nope.shape[:-1] + (MLA_ROPE_DIM,), F32))).astype(BF16)
    wvb = flat(_lay_v(wkvb[..., MLA_NOPE_DIM:])).astype(BF16)

    q_scale = GQA_HEAD_DIM ** -0.5 * LOG2E
    row = lambda v: v[:, None, :]
    return {
        "ln_mix": row(p["ln_mix"]), "wcat": wcat, "wqb": wqb, "wkb": wkb, "wvb": wvb,
        "gq": row(_lay_gqa(p["gqa_q_norm"] * q_scale)), "gk": row(_lay_gqa(p["gqa_k_norm"])),
        "gql": row(p["mla_q_norm"]), "gkv": row(p["mla_kv_norm"]),
        "on": row(jnp.concatenate([p["out_norm_gqa"], p["out_norm_mla"]], axis=-1)),
        "w_o": p["w_o"].astype(BF16), "ln_mem": row(p["ln_mem"]), "ln_mem_kv": row(p["ln_mem_kv"]),
        "w_mem_q": p["w_mem_q"].astype(BF16), "w_mem_kv": p["w_mem_kv"].astype(BF16),
        "w_mem_o": p["w_mem_o"].astype(BF16), "ln_ffn": row(p["ln_ffn"]),
        "w_router_t": _split_hi_lo(jnp.swapaxes(p["w_router"], 1, 2)),
    }


def _split_hi_lo(w):
    hi = w.astype(BF16)
    lo = (w - hi.astype(F32)).astype(BF16)
    return jnp.concatenate([hi, lo], axis=1)


def _tables(seq_len):
    cos_g, sin_g, cos_m, sin_m = _rope_tables(seq_len)
    mq_scale = MLA_QK_DIM ** -0.5 * LOG2E
    one_g = jnp.zeros((GQA_KV_HEADS, LANES), F32).at[:, ONE_LANE].set(1.0)
    one_m = jnp.zeros((MLA_HEADS, LANES), F32).at[:, ONE_LANE].set(1.0)
    return {"cos_g": cos_g, "sin_g": sin_g, "cos_m": cos_m, "sin_m": sin_m,
            "cos_mq": cos_m * mq_scale, "sin_mq": sin_m * mq_scale,
            "one_g": one_g.reshape(1, -1), "one_m": one_m.reshape(1, -1)}


def _pick(n, pref):
    t = min(n, pref)
    assert n % t == 0, (n, t)
    return t


def kernel(x, mem, ln_mix, w_in, gqa_q_norm, gqa_k_norm, mla_q_norm, mla_kv_norm, w_q_b, w_kv_b, out_norm_gqa,
           out_norm_mla, w_o, ln_mem, ln_mem_kv, w_mem_q, w_mem_kv, w_mem_o, ln_ffn, w_router, w_gate, w_up,
           w_down, ln_final):
    p = dict(ln_mix=ln_mix, w_in=w_in, gqa_q_norm=gqa_q_norm, gqa_k_norm=gqa_k_norm, mla_q_norm=mla_q_norm,
             mla_kv_norm=mla_kv_norm, w_q_b=w_q_b, w_kv_b=w_kv_b, out_norm_gqa=out_norm_gqa,
             out_norm_mla=out_norm_mla, w_o=w_o, ln_mem=ln_mem, ln_mem_kv=ln_mem_kv, w_mem_q=w_mem_q,
             w_mem_kv=w_mem_kv, w_mem_o=w_mem_o, ln_ffn=ln_ffn, w_router=w_router, w_gate=w_gate, w_up=w_up,
             w_down=w_down)
    B, S, D = x.shape
    depth = w_in.shape[0]
    E = N_EXPERTS
    assert S % LANES == 0 and S % GRID_W == 0
    cap = EC_CAPACITY_FACTOR * S // E
    chunks = S // LANES
    tabs = _tables(S)
    ln_final2 = ln_final.reshape(1, -1)
    ts_in, ts_post, tq, tk, tc = _pick(S, 256), _pick(S, 1024), _pick(S, 4096), _pick(S, 512), _pick(S, 512)
    sw = _stacked_weights(p)
    kv_mem = _mem_kv(mem, sw)

    for l in range(depth):
        q_all, k_all, v_all = _mixer_in(x, sw, l, tabs, ts_in)
        o = _attention(q_all, k_all, v_all, tq, tk)
        x2, h3, aff = _post_attn(o, x, kv_mem, sw, l, ts_post)
        slot, pos = _select(aff.reshape(B, E * chunks, LANES), cap, chunks)
        slot = slot.reshape(B, E, S)
        pstart = jnp.pad(pos[:, :, 0].reshape(B * E * chunks), (0, 1))
        y = _experts(pstart, slot.reshape(B, E, 1, S), aff.reshape(B, E, 1, S), h3, w_gate, w_up, w_down, l,
                     cap, chunks)
        x = _combine(pstart, x2, jnp.swapaxes(slot, 1, 2), y, ln_final2, cap, chunks, tc, final=(l == depth - 1))
    return x
```
